```python
import math
import jax, jax.numpy as jnp
from jax import lax
import numpy as np

D_MODEL = 2048
BATCH = 16
SEQ = 2048
DEPTH = 1

D_MIX = D_MODEL
A_HEADS = 8
A_QK_DIM = 64
A_V_DIM = 2 * A_QK_DIM
A_WIDTH = A_HEADS * A_V_DIM
B_HEADS = 8
B_LAT = 256
B_V_DIM = 128
B_WIDTH = B_HEADS * B_V_DIM
IDX_HEADS = 16
IDX_DIM = 64
TOPK_MAX = 256
N_BUCKETS = 32
MAX_DISTANCE = 128
N_BIAS_HEADS = A_HEADS + B_HEADS
Q_BLOCK = 128
EPS = 1e-6

IN_SIZES = (
    2 * A_HEADS * A_QK_DIM,
    2 * A_HEADS * A_QK_DIM,
    A_HEADS * A_V_DIM,
    A_WIDTH,
    B_HEADS * B_LAT,
    B_LAT,
    B_WIDTH,
    IDX_HEADS * IDX_DIM,
    IDX_DIM,
    IDX_HEADS,
)
N_IN = 2 * A_HEADS * A_QK_DIM * 2 + A_HEADS * A_V_DIM + A_WIDTH + B_HEADS * B_LAT + B_LAT + B_WIDTH + IDX_HEADS * IDX_DIM + IDX_DIM + IDX_HEADS

kernel_name = "hymba_diff_dsa_hybrid_layer"


def _split_points():
    pts, acc = [], 0
    for s in IN_SIZES[:-1]:
        acc += s
        pts.append(acc)
    return pts


def rmsnorm(x, g):
    xf = x.astype(jnp.float32)
    y = xf * lax.rsqrt(jnp.mean(xf * xf, axis=-1, keepdims=True) + EPS)
    return (y * g.astype(jnp.float32)).astype(x.dtype)


def t5_bucket(dist):
    n = jnp.maximum(dist, 0)
    max_exact = N_BUCKETS // 2
    nf = jnp.maximum(n, 1).astype(jnp.float32)
    large = max_exact + (jnp.log(nf / max_exact) / math.log(MAX_DISTANCE / max_exact)
                         * (N_BUCKETS - max_exact)).astype(jnp.int32)
    large = jnp.minimum(large, N_BUCKETS - 1)
    return jnp.where(n < max_exact, n, large)


def diff_attention(q, k, v, bias_tab, lam, sub_g, lam_init):
    B, T = q.shape[0], q.shape[1]
    scale = A_QK_DIM ** -0.5
    pos = jnp.arange(T)
    neg = jnp.finfo(jnp.float32).min

    def block(i):
        q0 = i * Q_BLOCK
        qb = lax.dynamic_slice_in_dim(q, q0, Q_BLOCK, axis=1)
        tq = q0 + jnp.arange(Q_BLOCK)
        dist = tq[:, None] - pos[None, :]
        bias = jnp.transpose(bias_tab[t5_bucket(dist)], (2, 0, 1))
        logits = jnp.einsum('btchd,bschd->bchts', qb, k).astype(jnp.float32) * scale + bias
        logits = jnp.where(dist >= 0, logits, neg)
        p = jax.nn.softmax(logits, axis=-1)
        a = p[:, 0] - lam * p[:, 1]
        return jnp.einsum('bhts,bshd->bthd', a.astype(v.dtype), v)

    o = lax.map(block, jnp.arange(T // Q_BLOCK))
    o = jnp.transpose(o, (1, 0, 2, 3, 4)).reshape(B, T, A_HEADS, A_V_DIM)
    return rmsnorm(o, sub_g) * (1.0 - lam_init)


def dsa_attention(q, ckv, iq, ik, iw, bias_tab, w_uv):
    B, T = q.shape[0], q.shape[1]
    S = ckv.shape[1]
    topk = min(TOPK_MAX, S // 4)
    scale = B_LAT ** -0.5
    idx_scale = IDX_DIM ** -0.5
    head_w_scale = IDX_HEADS ** -0.5
    pos = jnp.arange(S)
    neg = jnp.finfo(jnp.float32).min
    gather = jax.vmap(lambda c, s: c[s])

    def block(i):
        q0 = i * Q_BLOCK
        qb = lax.dynamic_slice_in_dim(q, q0, Q_BLOCK, axis=1)
        iqb = lax.dynamic_slice_in_dim(iq, q0, Q_BLOCK, axis=1)
        iwb = lax.dynamic_slice_in_dim(iw, q0, Q_BLOCK, axis=1)
        tq = q0 + jnp.arange(Q_BLOCK)
        rel = jax.nn.relu(jnp.einsum('bthd,bsd->bths', iqb, ik).astype(jnp.float32) * idx_scale)
        score = jnp.einsum('bths,bth->bts', rel, iwb.astype(jnp.float32) * head_w_scale)
        score = jnp.where(pos[None, None, :] <= tq[None, :, None], score, neg)
        _, sel = lax.top_k(score, topk)
        valid = sel <= tq[None, :, None]
        kv_sel = gather(ckv, sel)
        bias = bias_tab[t5_bucket(tq[None, :, None] - sel)]
        logits = (jnp.einsum('bthc,btkc->bhtk', qb, kv_sel).astype(jnp.float32) * scale
                  + jnp.transpose(bias, (0, 3, 1, 2)))
        logits = jnp.where(valid[:, None], logits, neg)
        p = jax.nn.softmax(logits, axis=-1)
        o_lat = jnp.einsum('bhtk,btkc->bthc', p.astype(kv_sel.dtype), kv_sel)
        return jnp.einsum('bthc,hcd->bthd', o_lat, w_uv)

    o = lax.map(block, jnp.arange(T // Q_BLOCK))
    return jnp.transpose(o, (1, 0, 2, 3, 4)).reshape(B, T, B_HEADS, B_V_DIM)


def setup_inputs(seed: int = 0) -> dict:
    key = jax.random.key(seed)
    ks = jax.random.split(key, 16)
    f32 = jnp.float32
    nrm = lambda k, shape, s: (jax.random.normal(k, shape, f32) * s)
    return {
        "x": nrm(ks[0], (BATCH, SEQ, D_MODEL), 1.0),
        "norm_pre_g": 1.0 + nrm(ks[1], (DEPTH, D_MODEL), 0.02),
        "w_in": nrm(ks[2], (DEPTH, D_MODEL, N_IN), D_MODEL ** -0.5),
        "lambda_q1": nrm(ks[3], (DEPTH, A_QK_DIM), 0.1),
        "lambda_k1": nrm(ks[4], (DEPTH, A_QK_DIM), 0.1),
        "lambda_q2": nrm(ks[5], (DEPTH, A_QK_DIM), 0.1),
        "lambda_k2": nrm(ks[6], (DEPTH, A_QK_DIM), 0.1),
        "subln_g": 1.0 + nrm(ks[7], (DEPTH, A_V_DIM), 0.02),
        "kv_norm_g": 1.0 + nrm(ks[8], (DEPTH, B_LAT), 0.02),
        "idx_k_norm_g": 1.0 + nrm(ks[9], (DEPTH, IDX_DIM), 0.02),
        "w_uv": nrm(ks[10], (DEPTH, B_HEADS, B_LAT, B_V_DIM), B_LAT ** -0.5),
        "rel_bias": nrm(ks[11], (N_BUCKETS, N_BIAS_HEADS), 0.1),
        "w_out": nrm(ks[12], (DEPTH, D_MIX, D_MODEL), D_MIX ** -0.5),
        "norm_post_g": 1.0 + nrm(ks[13], (DEPTH, D_MODEL), 0.02),
    }


def reference(x, norm_pre_g, w_in, lambda_q1, lambda_k1, lambda_q2, lambda_k2, subln_g,
              kv_norm_g, idx_k_norm_g, w_uv, rel_bias, w_out, norm_post_g):
    B, T, _ = x.shape
    pts = _split_points()
    bias_a = rel_bias[:, :A_HEADS]
    bias_b = rel_bias[:, A_HEADS:]
    for l in range(DEPTH):
        lam_init = 0.8 - 0.6 * math.exp(-0.3 * l)
        h = rmsnorm(x, norm_pre_g[l])
        proj = jnp.einsum('btd,dn->btn', h, w_in[l])
        qa, ka, va, za, qb, ckv, zb, iq, ik, iw = jnp.split(proj, pts, axis=-1)

        lam = (jnp.exp(jnp.sum(lambda_q1[l].astype(jnp.float32) * lambda_k1[l].astype(jnp.float32)))
               - jnp.exp(jnp.sum(lambda_q2[l].astype(jnp.float32) * lambda_k2[l].astype(jnp.float32)))
               + lam_init)
        o_a = diff_attention(qa.reshape(B, T, 2, A_HEADS, A_QK_DIM),
                             ka.reshape(B, T, 2, A_HEADS, A_QK_DIM),
                             va.reshape(B, T, A_HEADS, A_V_DIM),
                             bias_a, lam, subln_g[l], lam_init)
        o_a = o_a.reshape(B, T, A_WIDTH) * jax.nn.silu(za)

        o_b = dsa_attention(qb.reshape(B, T, B_HEADS, B_LAT),
                            rmsnorm(ckv, kv_norm_g[l]),
                            iq.reshape(B, T, IDX_HEADS, IDX_DIM),
                            rmsnorm(ik, idx_k_norm_g[l]),
                            iw, bias_b, w_uv[l])
        o_b = o_b.reshape(B, T, B_WIDTH) * jax.nn.silu(zb)

        y = jnp.einsum('btm,md->btd', jnp.concatenate([o_a, o_b], axis=-1), w_out[l])
        x = x + rmsnorm(y, norm_post_g[l])
    return x
```

```python
import functools
import math

import jax
import jax.numpy as jnp
import numpy as np
from jax import lax
from jax.experimental import pallas as pl
from jax.experimental.pallas import tpu as pltpu

F32 = jnp.float32
BF16 = jnp.bfloat16
I32 = jnp.int32

A_HEADS = 8
A_QK_DIM = 64
A_V_DIM = 2 * A_QK_DIM
A_WIDTH = A_HEADS * A_V_DIM
B_HEADS = 8
B_LAT = 256
B_V_DIM = 128
B_WIDTH = B_HEADS * B_V_DIM
IDX_HEADS = 16
IDX_DIM = 64
TOPK_MAX = 256
N_BUCKETS = 32
MAX_DISTANCE = 128
EPS = 1e-6

LANES = 128
ATT_TILE = 256
PROJ_TM = 1024
PROJ_TN = 1024
OUT_TM = 512
SMALL_W = 512
INT_MIN = np.int32(-2 ** 31)
M_INIT = -1e30

_QA, _KA, _VA, _ZA, _QB, _ZB, _IQ, _MAIN_END = 0, 8, 16, 24, 32, 48, 56, 64
MAIN_W = _MAIN_END * LANES


def _dot(a, b):
    return jnp.dot(a, b, preferred_element_type=F32)


def _dot_nt(a, b):
    return lax.dot_general(a, b, (((1,), (1,)), ((), ())), preferred_element_type=F32)


def _rows8_sum(x):
    return x.reshape(x.shape[0] // 8, 8, x.shape[1]).sum(axis=0)


def _proj_kernel(x_ref, g_ref, w_ref, ws_ref, kvg_ref, ikg_ref, main_ref, ckv_ref, ik_ref, iw_ref, h_ref):
    @pl.when(pl.program_id(1) == 0)
    def _():
        x = x_ref[...]
        ms = jnp.mean(x * x, axis=-1, keepdims=True)
        hb = (x * lax.rsqrt(ms + EPS) * g_ref[...]).astype(BF16)
        h_ref[...] = hb
        s = _dot(hb, ws_ref[...])
        ckv = s[:, :B_LAT]
        ckv_ms = jnp.mean(ckv * ckv, axis=-1, keepdims=True)
        ckv_ref[...] = (ckv * lax.rsqrt(ckv_ms + EPS) * kvg_ref[...]).astype(BF16)
        ik2 = s[:, B_LAT:B_LAT + 2 * IDX_DIM]
        ik_ms = jnp.sum(ik2 * ik2, axis=-1, keepdims=True) * (1.0 / (2 * IDX_DIM))
        ik_ref[...] = (ik2 * lax.rsqrt(ik_ms + EPS) * ikg_ref[...]).astype(BF16)
        iw_ref[...] = s[:, B_LAT + 2 * IDX_DIM:B_LAT + 2 * IDX_DIM + LANES]

    main_ref[...] = _dot(h_ref[...], w_ref[...]).astype(BF16)


def _input_projection(x2, g, w_main, w_small, kv_g, ik_g2):
    m, d = x2.shape
    tm = min(PROJ_TM, m)
    return pl.pallas_call(
        _proj_kernel,
        grid=(m // tm, MAIN_W // PROJ_TN),
        in_specs=[
            pl.BlockSpec((tm, d), lambda i, j: (i, 0)),
            pl.BlockSpec((1, d), lambda i, j: (0, 0)),
            pl.BlockSpec((d, PROJ_TN), lambda i, j: (0, j)),
            pl.BlockSpec((d, SMALL_W), lambda i, j: (0, 0)),
            pl.BlockSpec((1, B_LAT), lambda i, j: (0, 0)),
            pl.BlockSpec((1, 2 * IDX_DIM), lambda i, j: (0, 0)),
        ],
        out_specs=[
            pl.BlockSpec((tm, PROJ_TN), lambda i, j: (i, j)),
            pl.BlockSpec((tm, B_LAT), lambda i, j: (i, 0)),
            pl.BlockSpec((tm, 2 * IDX_DIM), lambda i, j: (i, 0)),
            pl.BlockSpec((tm, LANES), lambda i, j: (i, 0)),
        ],
        out_shape=[
            jax.ShapeDtypeStruct((m, MAIN_W), BF16),
            jax.ShapeDtypeStruct((m, B_LAT), BF16),
            jax.ShapeDtypeStruct((m, 2 * IDX_DIM), BF16),
            jax.ShapeDtypeStruct((m, LANES), F32),
        ],
        scratch_shapes=[pltpu.VMEM((tm, d), BF16)],
        compiler_params=pltpu.CompilerParams(
            dimension_semantics=("parallel", "arbitrary"), vmem_limit_bytes=52 * 2 ** 20),
        name="input_projection",
    )(x2, g, w_main, w_small, kv_g, ik_g2)


def _attn_a_kernel(q_ref, k_ref, v_ref, z_ref, bias_ref, lq1_ref, lk1_ref, lq2_ref, lk2_ref, g_ref,
                   o_ref, qs_ref, m_ref, l_ref, acc_ref, *, lam_init, n_blocks):
    t = ATT_TILE
    lam = (jnp.exp(jnp.sum(lq1_ref[...] * lk1_ref[...], axis=-1, keepdims=True))
           - jnp.exp(jnp.sum(lq2_ref[...] * lk2_ref[...], axis=-1, keepdims=True)) + lam_init)
    lane = lax.broadcasted_iota(I32, (t, LANES), 1)
    zero = jnp.zeros((t, LANES), BF16)

    def q_block(i, carry):
        rows = pl.ds(pl.multiple_of(i * t, t), t)
        q = q_ref[rows, :] * (A_QK_DIM ** -0.5)
        qs_ref[0:t, :] = jnp.where(lane < A_QK_DIM, q, zero)
        qs_ref[t:2 * t, :] = jnp.where(lane >= A_QK_DIM, q, zero)
        m_ref[...] = jnp.full(m_ref.shape, M_INIT, F32)
        l_ref[...] = jnp.zeros(l_ref.shape, F32)
        acc_ref[...] = jnp.zeros(acc_ref.shape, F32)

        def kv_chunk(j, c):
            keys = pl.ds(pl.multiple_of(j * t, t), t)
            kj = k_ref[keys, :]
            vj = v_ref[keys, :]
            bt = bias_ref[jnp.minimum(i - j, 2)]
            s = _dot_nt(qs_ref[...], kj)
            for c_ in range(2):
                sc = s[c_ * t:(c_ + 1) * t] + bt
                m_old = m_ref[c_]
                m_new = jnp.maximum(m_old, jnp.max(sc, axis=-1, keepdims=True))
                alpha = jnp.exp(m_old - m_new)
                p = jnp.exp(sc - m_new)
                l_ref[c_] = alpha * l_ref[c_] + jnp.sum(p, axis=-1, keepdims=True)
                acc_ref[c_] = alpha * acc_ref[c_] + _dot(p.astype(BF16), vj)
                m_ref[c_] = m_new
            return c

        lax.fori_loop(0, i + 1, kv_chunk, 0)

        a = acc_ref[0] / l_ref[0] - lam * (acc_ref[1] / l_ref[1])
        y = a * lax.rsqrt(jnp.mean(a * a, axis=-1, keepdims=True) + EPS) * g_ref[...] * (1.0 - lam_init)
        z = z_ref[rows, :].astype(F32)
        o_ref[rows, :] = (y * (z / (1.0 + jnp.exp(-z)))).astype(BF16)
        return carry

    lax.fori_loop(0, n_blocks, q_block, 0)


def _attention_a(main, bias_tiles, lq1, lk1, lq2, lk2, sub_g, batch, seq, lam_init):
    t = ATT_TILE
    vec = lambda n: pl.BlockSpec((1, n), lambda b, h: (0, 0))
    col = lambda off: pl.BlockSpec((seq, LANES), lambda b, h: (b, off + h))
    return pl.pallas_call(
        functools.partial(_attn_a_kernel, lam_init=lam_init, n_blocks=seq // t),
        grid=(batch, A_HEADS),
        in_specs=[
            col(_QA), col(_KA), col(_VA), col(_ZA),
            pl.BlockSpec((None, 3, t, t), lambda b, h: (h, 0, 0, 0)),
            vec(A_QK_DIM), vec(A_QK_DIM), vec(A_QK_DIM), vec(A_QK_DIM), vec(A_V_DIM),
        ],
        out_specs=pl.BlockSpec((seq, LANES), lambda b, h: (b, h)),
        out_shape=jax.ShapeDtypeStruct((batch * seq, A_WIDTH), BF16),
        scratch_shapes=[
            pltpu.VMEM((2 * t, LANES), BF16),
            pltpu.VMEM((2, t, 1), F32),
            pltpu.VMEM((2, t, 1), F32),
            pltpu.VMEM((2, t, A_V_DIM), F32),
        ],
        compiler_params=pltpu.CompilerParams(
            dimension_semantics=("parallel", "arbitrary"), vmem_limit_bytes=40 * 2 ** 20),
        name="attention_a",
    )(main, main, main, main, bias_tiles, lq1, lk1, lq2, lk2, sub_g)


def _attn_b_kernel(qb_ref, iq_ref, zb_ref, ckv_ref, ckvt_ref, ik_ref, iwt_ref, bias_ref, wuvt_ref,
                   o_ref, keys_ref, iqm_ref, qs_ref, m_ref, l_ref, acc_ref, j_ref, *, topk, seq):
    t = ATT_TILE
    i = pl.program_id(1)
    n_chunks = i + 1

    lane = lax.broadcasted_iota(I32, (t, LANES), 1)
    zero = jnp.zeros((t, LANES), BF16)
    for hp in range(IDX_HEADS // 2):
        pair = iq_ref[:, hp * LANES:(hp + 1) * LANES]
        iqm_ref[2 * hp] = jnp.where(lane < IDX_DIM, pair, zero)
        iqm_ref[2 * hp + 1] = jnp.where(lane >= IDX_DIM, pair, zero)
    qs_ref[...] = qb_ref[...] * (B_LAT ** -0.5)
    wt = iwt_ref[...] * ((IDX_HEADS ** -0.5) * (IDX_DIM ** -0.5))

    key_minus_query = (lax.broadcasted_iota(I32, (t, t), 0) - lax.broadcasted_iota(I32, (t, t), 1))
    key_row = lax.broadcasted_iota(I32, (t, t), 0)

    def idx_chunk(j, c):
        rows = pl.ds(pl.multiple_of(j * t, t), t)
        ikj = ik_ref[rows, :]
        score = jnp.zeros((t, t), F32)
        for h in range(IDX_HEADS):
            d = _dot_nt(ikj, iqm_ref[h])
            score = score + wt[h:h + 1, :] * jnp.maximum(d, 0.0)
        bits = lax.bitcast_convert_type(score, I32)
        key = bits ^ ((bits >> 31) & np.int32(0x7FFFFFFF))
        causal = key_minus_query <= (i - j) * t
        keys_ref[rows, :] = jnp.where(causal, key, INT_MIN)
        return c

    lax.fori_loop(0, n_chunks, idx_chunk, 0)

    def count(pred_fn):
        def body(j, c8):
            kj = keys_ref[pl.ds(pl.multiple_of(j * t, t), t), :]
            return c8 + _rows8_sum(pred_fn(kj, j).astype(I32))
        c8 = lax.fori_loop(0, n_chunks, body, jnp.zeros((8, t), I32))
        return jnp.sum(c8, axis=0, keepdims=True)

    def bit_step(b, carry):
        cu, cnt_at = carry
        trial_u = cu | lax.shift_left(np.int32(1), np.int32(31) - b)
        trial = trial_u ^ INT_MIN
        cnt = count(lambda kj, j: kj >= trial)
        ok = cnt >= topk
        return jnp.where(ok, trial_u, cu), jnp.where(ok, cnt, cnt_at)

    cu, cnt_ge = lax.fori_loop(0, 32, bit_step, (jnp.zeros((1, t), I32), jnp.zeros((1, t), I32)))
    thr = cu ^ INT_MIN
    thr_eq = jnp.maximum(thr, INT_MIN + np.int32(1))

    idx_bits = max(1, (seq - 1).bit_length())
    j_ref[...] = jnp.full((1, t), 2 ** idx_bits - 1, I32)

    @pl.when(jnp.max((cnt_ge - topk).astype(F32)) > 0.0)
    def _():
        need = topk - count(lambda kj, j: kj > thr)

        def pos_step(b, jlim):
            trial = jlim + lax.shift_left(np.int32(1), np.int32(idx_bits - 1) - b)
            cnt = count(lambda kj, j: (kj == thr_eq) & (key_row + j * t < trial))
            return jnp.where(cnt < need, trial, jlim)

        j_ref[...] = lax.fori_loop(0, idx_bits, pos_step, jnp.zeros((1, t), I32))

    j_lim = j_ref[...]

    m_ref[...] = jnp.full(m_ref.shape, M_INIT, F32)
    l_ref[...] = jnp.zeros(l_ref.shape, F32)
    acc_ref[...] = jnp.zeros(acc_ref.shape, F32)

    def att_chunk(j, c):
        rows = pl.ds(pl.multiple_of(j * t, t), t)
        kj = keys_ref[rows, :]
        sel = (kj > thr) | ((kj == thr_eq) & (key_row + j * t <= j_lim))
        ckvj = ckv_ref[rows, :]
        ckvtj = ckvt_ref[j]
        bidx = jnp.minimum(i - j, 2)
        for h in range(B_HEADS):
            lg = _dot_nt(ckvj, qs_ref[:, h * B_LAT:(h + 1) * B_LAT]) + bias_ref[h, bidx]
            lg = jnp.where(sel, lg, -jnp.inf)
            m_old = m_ref[h]
            m_new = jnp.maximum(m_old, jnp.max(lg, axis=0, keepdims=True))
            alpha = jnp.exp(m_old - m_new)
            p = jnp.exp(lg - m_new)
            l_ref[h] = alpha * l_ref[h] + jnp.sum(p, axis=0, keepdims=True)
            acc_ref[h] = alpha * acc_ref[h] + _dot(ckvtj, p.astype(BF16))
            m_ref[h] = m_new
        return c

    lax.fori_loop(0, n_chunks, att_chunk, 0)

    for h in range(B_HEADS):
        o_lat_t = (acc_ref[h] * (1.0 / l_ref[h])).astype(BF16)
        o_t = _dot(wuvt_ref[h], o_lat_t)
        z = zb_ref[:, h * B_V_DIM:(h + 1) * B_V_DIM].astype(F32)
        o_ref[:, h * B_V_DIM:(h + 1) * B_V_DIM] = (o_t.T * (z / (1.0 + jnp.exp(-z)))).astype(BF16)


def _attention_b(main, ckv, ckv_t, ik2, iw_t, bias_tiles_t, w_uv_t, batch, seq, topk):
    t = ATT_TILE
    nq = seq // t
    return pl.pallas_call(
        functools.partial(_attn_b_kernel, topk=topk, seq=seq),
        grid=(batch, nq),
        in_specs=[
            pl.BlockSpec((t, B_HEADS * B_LAT), lambda b, i: (b * nq + i, _QB * LANES // (B_HEADS * B_LAT))),
            pl.BlockSpec((t, IDX_HEADS * IDX_DIM), lambda b, i: (b * nq + i, _IQ * LANES // (IDX_HEADS * IDX_DIM))),
            pl.BlockSpec((t, B_WIDTH), lambda b, i: (b * nq + i, _ZB * LANES // B_WIDTH)),
            pl.BlockSpec((seq, B_LAT), lambda b, i: (b, 0)),
            pl.BlockSpec((None, nq, B_LAT, t), lambda b, i: (b, 0, 0, 0)),
            pl.BlockSpec((seq, 2 * IDX_DIM), lambda b, i: (b, 0)),
            pl.BlockSpec((IDX_HEADS, t), lambda b, i: (0, b * nq + i)),
            pl.BlockSpec((B_HEADS, 3, t, t), lambda b, i: (0, 0, 0, 0)),
            pl.BlockSpec((B_HEADS, B_V_DIM, B_LAT), lambda b, i: (0, 0, 0)),
        ],
        out_specs=pl.BlockSpec((t, B_WIDTH), lambda b, i: (b * nq + i, 0)),
        out_shape=jax.ShapeDtypeStruct((batch * seq, B_WIDTH), BF16),
        scratch_shapes=[
            pltpu.VMEM((seq, t), I32),
            pltpu.VMEM((IDX_HEADS, t, LANES), BF16),
            pltpu.VMEM((t, B_HEADS * B_LAT), BF16),
            pltpu.VMEM((B_HEADS, 1, t), F32),
            pltpu.VMEM((B_HEADS, 1, t), F32),
            pltpu.VMEM((B_HEADS, B_LAT, t), F32),
            pltpu.VMEM((1, t), I32),
        ],
        compiler_params=pltpu.CompilerParams(
            dimension_semantics=("parallel", "arbitrary"), vmem_limit_bytes=52 * 2 ** 20),
        name="attention_b",
    )(main, main, main, ckv, ckv_t, ik2, iw_t, bias_tiles_t, w_uv_t)


def _out_kernel(oa_ref, ob_ref, x_ref, w_ref, g_ref, o_ref):
    y = _dot(oa_ref[...], w_ref[0:A_WIDTH, :]) + _dot(ob_ref[...], w_ref[A_WIDTH:A_WIDTH + B_WIDTH, :])
    ms = jnp.mean(y * y, axis=-1, keepdims=True)
    o_ref[...] = x_ref[...] + y * lax.rsqrt(ms + EPS) * g_ref[...]


def _output_projection(oa, ob, x2, w_out, g):
    m, d = x2.shape
    tm = min(OUT_TM, m)
    return pl.pallas_call(
        _out_kernel,
        grid=(m // tm,),
        in_specs=[
            pl.BlockSpec((tm, A_WIDTH), lambda i: (i, 0)),
            pl.BlockSpec((tm, B_WIDTH), lambda i: (i, 0)),
            pl.BlockSpec((tm, d), lambda i: (i, 0)),
            pl.BlockSpec((A_WIDTH + B_WIDTH, d), lambda i: (0, 0)),
            pl.BlockSpec((1, d), lambda i: (0, 0)),
        ],
        out_specs=pl.BlockSpec((tm, d), lambda i: (i, 0)),
        out_shape=jax.ShapeDtypeStruct((m, d), F32),
        compiler_params=pltpu.CompilerParams(
            dimension_semantics=("parallel",), vmem_limit_bytes=52 * 2 ** 20),
        name="output_projection",
    )(oa, ob, x2, w_out, g)


def _t5_bucket(dist):
    n = jnp.maximum(dist, 0)
    max_exact = N_BUCKETS // 2
    nf = jnp.maximum(n, 1).astype(F32)
    large = max_exact + (jnp.log(nf / max_exact) / math.log(MAX_DISTANCE / max_exact)
                         * (N_BUCKETS - max_exact)).astype(I32)
    large = jnp.minimum(large, N_BUCKETS - 1)
    return jnp.where(n < max_exact, n, large)


def _bias_tiles(bias_tab):
    t = ATT_TILE
    assert t > MAX_DISTANCE
    r = jnp.arange(t)[:, None]
    c = jnp.arange(t)[None, :]
    dist = jnp.stack([d * t + r - c for d in range(3)])
    return jnp.transpose(bias_tab[_t5_bucket(dist)], (3, 0, 1, 2)), dist


def _split_w_in(w):
    d = w.shape[0]
    sizes = (2 * A_HEADS * A_QK_DIM, 2 * A_HEADS * A_QK_DIM, A_WIDTH, A_WIDTH, B_HEADS * B_LAT, B_LAT,
             B_WIDTH, IDX_HEADS * IDX_DIM, IDX_DIM, IDX_HEADS)
    assert sum(sizes) == w.shape[1]
    offs = np.cumsum((0,) + sizes)
    qa, ka, va, za, qb, ckv, zb, iq, ik, iw = (w[:, offs[n]:offs[n + 1]] for n in range(len(sizes)))

    def by_head(m):
        return m.reshape(d, 2, A_HEADS, A_QK_DIM).transpose(0, 2, 1, 3).reshape(d, 2 * A_HEADS * A_QK_DIM)

    main = jnp.concatenate([by_head(qa), by_head(ka), va, za, qb, zb, iq], axis=1)
    pad = jnp.zeros((d, SMALL_W - B_LAT - 2 * IDX_DIM - IDX_HEADS), w.dtype)
    small = jnp.concatenate([ckv, ik, ik, iw, pad], axis=1)
    assert main.shape[1] == MAIN_W and small.shape[1] == SMALL_W
    return main.astype(BF16), small.astype(BF16)


def kernel(x, norm_pre_g, w_in, lambda_q1, lambda_k1, lambda_q2, lambda_k2, subln_g, kv_norm_g, idx_k_norm_g,
           w_uv, rel_bias, w_out, norm_post_g):
    batch, seq, d_model = x.shape
    t = ATT_TILE
    assert seq % t == 0 and d_model % LANES == 0
    nq = seq // t
    topk = min(TOPK_MAX, seq // 4)
    row = lambda v: v.reshape(1, -1).astype(F32)

    bias_a, dist = _bias_tiles(rel_bias[:, :A_HEADS])
    bias_a = jnp.where(dist[None] >= 0, bias_a, -jnp.inf)
    bias_b, _ = _bias_tiles(rel_bias[:, A_HEADS:])
    bias_b_t = jnp.swapaxes(bias_b, 2, 3)

    x2 = x.reshape(batch * seq, d_model)
    for l in range(w_in.shape[0]):
        lam_init = 0.8 - 0.6 * math.exp(-0.3 * l)
        w_main, w_small = _split_w_in(w_in[l])
        ik_g2 = jnp.concatenate([idx_k_norm_g[l], idx_k_norm_g[l]])
        main, ckv, ik2, iw = _input_projection(x2, row(norm_pre_g[l]), w_main, w_small,
                                               row(kv_norm_g[l]), row(ik_g2))
        o_a = _attention_a(main, bias_a, row(lambda_q1[l]), row(lambda_k1[l]), row(lambda_q2[l]),
                           row(lambda_k2[l]), row(subln_g[l]), batch, seq, lam_init)
        ckv_t = jnp.transpose(ckv.reshape(batch, nq, t, B_LAT), (0, 1, 3, 2))
        iw_t = jnp.transpose(iw[:, :IDX_HEADS])
        w_uv_t = jnp.swapaxes(w_uv[l], 1, 2).astype(BF16)
        o_b = _attention_b(main, ckv, ckv_t, ik2, iw_t, bias_b_t, w_uv_t, batch, seq, topk)
        x2 = _output_projection(o_a, o_b, x2, w_out[l].astype(BF16), row(norm_post_g[l]))
    return x2.reshape(batch, seq, d_model)
```

```python
import functools
import math

import jax
import jax.numpy as jnp
import numpy as np
from jax import lax
from jax.experimental import pallas as pl
from jax.experimental.pallas import tpu as pltpu

F32 = jnp.float32
BF16 = jnp.bfloat16
I32 = jnp.int32

A_HEADS = 8
A_QK_DIM = 64
A_V_DIM = 2 * A_QK_DIM
A_WIDTH = A_HEADS * A_V_DIM
B_HEADS = 8
B_LAT = 256
B_V_DIM = 128
B_WIDTH = B_HEADS * B_V_DIM
IDX_HEADS = 16
IDX_DIM = 64
TOPK_MAX = 256
N_BUCKETS = 32
MAX_DISTANCE = 128
EPS = 1e-6

LANES = 128
ATT_TILE = 256
PROJ_TM = 1024
PROJ_TN = 1024
OUT_TM = 512
SMALL_W = 384
INT_MIN = np.int32(-2 ** 31)
M_INIT = -1e30

_QB, _QA, _KA, _ZA, _ZB, _IQ, _MAIN_END = 0, 16, 24, 32, 40, 48, 56
MAIN_W = _MAIN_END * LANES


def _dot(a, b):
    return jnp.dot(a, b, preferred_element_type=F32)


def _dot_nt(a, b):
    return lax.dot_general(a, b, (((1,), (1,)), ((), ())), preferred_element_type=F32)


def _rows8_sum(x):
    return x.reshape(x.shape[0] // 8, 8, x.shape[1]).sum(axis=0)


def _proj_kernel(x_ref, g_ref, w_ref, ws_ref, wt_ref, kvg_ref, kvg_col_ref, ikg_ref,
                 main_ref, ckv_ref, ik_ref, vt_ref, ckvt_ref, iwt_ref, h_ref, *, n_main):
    j = pl.program_id(1)
    t = ATT_TILE

    @pl.when(j == 0)
    def _():
        x = x_ref[...]
        ms = jnp.mean(x * x, axis=-1, keepdims=True)
        hb = (x * lax.rsqrt(ms + EPS) * g_ref[...]).astype(BF16)
        h_ref[...] = hb
        s = _dot(hb, ws_ref[...])
        ckv = s[:, :B_LAT]
        ckv_ms = jnp.mean(ckv * ckv, axis=-1, keepdims=True)
        ckv_ref[...] = (ckv * lax.rsqrt(ckv_ms + EPS) * kvg_ref[...]).astype(BF16)
        ik2 = s[:, B_LAT:B_LAT + 2 * IDX_DIM]
        ik_ms = jnp.sum(ik2 * ik2, axis=-1, keepdims=True) * (1.0 / (2 * IDX_DIM))
        ik_ref[...] = (ik2 * lax.rsqrt(ik_ms + EPS) * ikg_ref[...]).astype(BF16)

    @pl.when(j < n_main)
    def _():
        main_ref[...] = _dot(h_ref[...], w_ref[...]).astype(BF16)

    @pl.when(j == n_main)
    def _():
        h = h_ref[...]
        vt = _dot_nt(wt_ref[0:A_WIDTH, :], h).astype(BF16)
        rest = _dot_nt(wt_ref[A_WIDTH:, :], h)
        ckvt = rest[:B_LAT]
        ckvt_ms = jnp.mean(ckvt * ckvt, axis=0, keepdims=True)
        ckvt = (ckvt * lax.rsqrt(ckvt_ms + EPS) * kvg_col_ref[...]).astype(BF16)
        for c in range(vt_ref.shape[0]):
            vt_ref[c] = vt[:, c * t:(c + 1) * t]
            ckvt_ref[c] = ckvt[:, c * t:(c + 1) * t]
        iwt_ref[...] = rest[B_LAT:B_LAT + IDX_HEADS]


def _input_projection(x2, g, w_main, w_small, w_t, kv_g, ik_g2):
    m, d = x2.shape
    t = ATT_TILE
    tm = min(PROJ_TM, m)
    n_main = MAIN_W // PROJ_TN
    once = dict(pipeline_mode=pl.Buffered(1))
    return pl.pallas_call(
        functools.partial(_proj_kernel, n_main=n_main),
        grid=(m // tm, n_main + 1),
        in_specs=[
            pl.BlockSpec((tm, d), lambda i, j: (i, 0)),
            pl.BlockSpec((1, d), lambda i, j: (0, 0)),
            pl.BlockSpec((d, PROJ_TN), lambda i, j: (0, jnp.minimum(j, n_main - 1))),
            pl.BlockSpec((d, SMALL_W), lambda i, j: (0, 0), **once),
            pl.BlockSpec((A_WIDTH + B_LAT + IDX_HEADS, d), lambda i, j: (0, 0), **once),
            pl.BlockSpec((1, B_LAT), lambda i, j: (0, 0)),
            pl.BlockSpec((B_LAT, 1), lambda i, j: (0, 0)),
            pl.BlockSpec((1, 2 * IDX_DIM), lambda i, j: (0, 0)),
        ],
        out_specs=[
            pl.BlockSpec((tm, PROJ_TN), lambda i, j: (i, jnp.minimum(j, n_main - 1))),
            pl.BlockSpec((tm, B_LAT), lambda i, j: (i, 0)),
            pl.BlockSpec((tm, 2 * IDX_DIM), lambda i, j: (i, 0)),
            pl.BlockSpec((tm // t, A_WIDTH, t), lambda i, j: (i, 0, 0)),
            pl.BlockSpec((tm // t, B_LAT, t), lambda i, j: (i, 0, 0)),
            pl.BlockSpec((IDX_HEADS, tm), lambda i, j: (0, i)),
        ],
        out_shape=[
            jax.ShapeDtypeStruct((m, MAIN_W), BF16),
            jax.ShapeDtypeStruct((m, B_LAT), BF16),
            jax.ShapeDtypeStruct((m, 2 * IDX_DIM), BF16),
            jax.ShapeDtypeStruct((m // t, A_WIDTH, t), BF16),
            jax.ShapeDtypeStruct((m // t, B_LAT, t), BF16),
            jax.ShapeDtypeStruct((IDX_HEADS, m), F32),
        ],
        scratch_shapes=[pltpu.VMEM((tm, d), BF16)],
        compiler_params=pltpu.CompilerParams(
            dimension_semantics=("parallel", "arbitrary"), vmem_limit_bytes=56 * 2 ** 20),
        name="input_projection",
    )(x2, g, w_main, w_small, w_t, kv_g, kv_g.reshape(-1, 1), ik_g2)


def _attn_a_kernel(q_ref, k_ref, vt_ref, z_ref, bias_ref, lq1_ref, lk1_ref, lq2_ref, lk2_ref, g_ref,
                   o_ref, qs_ref, m_ref, l_ref, acc_ref, *, lam_init, n_blocks):
    t = ATT_TILE
    lam = (jnp.exp(jnp.sum(lq1_ref[...] * lk1_ref[...], axis=-1, keepdims=True))
           - jnp.exp(jnp.sum(lq2_ref[...] * lk2_ref[...], axis=-1, keepdims=True)) + lam_init)
    lane = lax.broadcasted_iota(I32, (t, LANES), 1)
    zero = jnp.zeros((t, LANES), BF16)

    def q_block(i, carry):
        rows = pl.ds(pl.multiple_of(i * t, t), t)
        q = q_ref[rows, :] * (A_QK_DIM ** -0.5)
        qs_ref[0:t, :] = jnp.where(lane < A_QK_DIM, q, zero)
        qs_ref[t:2 * t, :] = jnp.where(lane >= A_QK_DIM, q, zero)
        m_ref[...] = jnp.full(m_ref.shape, M_INIT, F32)
        l_ref[...] = jnp.zeros(l_ref.shape, F32)
        acc_ref[...] = jnp.zeros(acc_ref.shape, F32)

        def kv_chunk(j, c):
            kj = k_ref[pl.ds(pl.multiple_of(j * t, t), t), :]
            vtj = vt_ref[j]
            bt = bias_ref[jnp.minimum(i - j, 2)]
            s = _dot_nt(kj, qs_ref[...])
            for c_ in range(2):
                cols = slice(c_ * t, (c_ + 1) * t)
                sc = s[:, cols] + bt
                m_old = m_ref[:, cols]
                m_new = jnp.maximum(m_old, jnp.max(sc, axis=0, keepdims=True))
                alpha = jnp.exp(m_old - m_new)
                p = jnp.exp(sc - m_new)
                l_ref[:, cols] = alpha * l_ref[:, cols] + jnp.sum(p, axis=0, keepdims=True)
                acc_ref[:, cols] = alpha * acc_ref[:, cols] + _dot(vtj, p.astype(BF16))
                m_ref[:, cols] = m_new
            return c

        lax.fori_loop(0, i + 1, kv_chunk, 0)

        inv_l = 1.0 / l_ref[...]
        a_t = acc_ref[:, 0:t] * inv_l[:, 0:t] - lam * (acc_ref[:, t:2 * t] * inv_l[:, t:2 * t])
        y_t = a_t * lax.rsqrt(jnp.mean(a_t * a_t, axis=0, keepdims=True) + EPS)
        z = z_ref[rows, :].astype(F32)
        o_ref[rows, :] = (y_t.T * g_ref[...] * (1.0 - lam_init) * (z / (1.0 + jnp.exp(-z)))).astype(BF16)
        return carry

    lax.fori_loop(0, n_blocks, q_block, 0)


def _attention_a(main, v_t, bias_tiles_t, lq1, lk1, lq2, lk2, sub_g, batch, seq, lam_init):
    t = ATT_TILE
    nq = seq // t
    vec = lambda n: pl.BlockSpec((1, n), lambda b, h: (0, 0))
    col = lambda off: pl.BlockSpec((seq, LANES), lambda b, h: (b, off + h))
    return pl.pallas_call(
        functools.partial(_attn_a_kernel, lam_init=lam_init, n_blocks=nq),
        grid=(batch, A_HEADS),
        in_specs=[
            col(_QA), col(_KA),
            pl.BlockSpec((nq, A_V_DIM, t), lambda b, h: (b, h, 0)),
            col(_ZA),
            pl.BlockSpec((None, 3, t, t), lambda b, h: (h, 0, 0, 0)),
            vec(A_QK_DIM), vec(A_QK_DIM), vec(A_QK_DIM), vec(A_QK_DIM), vec(A_V_DIM),
        ],
        out_specs=pl.BlockSpec((seq, LANES), lambda b, h: (b, h)),
        out_shape=jax.ShapeDtypeStruct((batch * seq, A_WIDTH), BF16),
        scratch_shapes=[
            pltpu.VMEM((2 * t, LANES), BF16),
            pltpu.VMEM((1, 2 * t), F32),
            pltpu.VMEM((1, 2 * t), F32),
            pltpu.VMEM((A_V_DIM, 2 * t), F32),
        ],
        compiler_params=pltpu.CompilerParams(
            dimension_semantics=("parallel", "arbitrary"), vmem_limit_bytes=40 * 2 ** 20),
        name="attention_a",
    )(main, main, v_t, main, bias_tiles_t, lq1, lk1, lq2, lk2, sub_g)


def _attn_b_kernel(qb_ref, iq_ref, zb_ref, ckv_ref, ckvt_ref, ik_ref, iwt_ref, bias_ref, wuvt_ref,
                   o_ref, keys_ref, iqm_ref, qs_ref, m_ref, l_ref, acc_ref, j_ref, *, topk, seq):
    t = ATT_TILE
    i = pl.program_id(1)
    n_chunks = i + 1

    lane = lax.broadcasted_iota(I32, (t, LANES), 1)
    zero = jnp.zeros((t, LANES), BF16)
    for hp in range(IDX_HEADS // 2):
        pair = iq_ref[:, hp * LANES:(hp + 1) * LANES]
        iqm_ref[2 * hp] = jnp.where(lane < IDX_DIM, pair, zero)
        iqm_ref[2 * hp + 1] = jnp.where(lane >= IDX_DIM, pair, zero)
    qs_ref[...] = qb_ref[...] * (B_LAT ** -0.5)
    wt = iwt_ref[...] * ((IDX_HEADS ** -0.5) * (IDX_DIM ** -0.5))

    key_minus_query = (lax.broadcasted_iota(I32, (t, t), 0) - lax.broadcasted_iota(I32, (t, t), 1))
    key_row = lax.broadcasted_iota(I32, (t, t), 0)

    def idx_chunk(j, c):
        rows = pl.ds(pl.multiple_of(j * t, t), t)
        ikj = ik_ref[rows, :]
        score = jnp.zeros((t, t), F32)
        for h in range(IDX_HEADS):
            d = _dot_nt(ikj, iqm_ref[h])
            score = score + wt[h:h + 1, :] * jnp.maximum(d, 0.0)
        bits = lax.bitcast_convert_type(score, I32)
        key = bits ^ ((bits >> 31) & np.int32(0x7FFFFFFF))
        causal = key_minus_query <= (i - j) * t
        keys_ref[rows, :] = jnp.where(causal, key, INT_MIN)
        return c

    lax.fori_loop(0, n_chunks, idx_chunk, 0)

    def count(pred_fn):
        def body(j, c8):
            kj = keys_ref[pl.ds(pl.multiple_of(j * t, t), t), :]
            return c8 + _rows8_sum(pred_fn(kj, j).astype(I32))
        c8 = lax.fori_loop(0, n_chunks, body, jnp.zeros((8, t), I32))
        return jnp.sum(c8, axis=0, keepdims=True)

    def bit_step(b, carry):
        cu, cnt_at = carry
        trial_u = cu | lax.shift_left(np.int32(1), np.int32(31) - b)
        trial = trial_u ^ INT_MIN
        cnt = count(lambda kj, j: kj >= trial)
        ok = cnt >= topk
        return jnp.where(ok, trial_u, cu), jnp.where(ok, cnt, cnt_at)

    cu, cnt_ge = lax.fori_loop(0, 32, bit_step, (jnp.zeros((1, t), I32), jnp.zeros((1, t), I32)))
    thr = cu ^ INT_MIN
    thr_eq = jnp.maximum(thr, INT_MIN + np.int32(1))

    idx_bits = max(1, (seq - 1).bit_length())
    j_ref[...] = jnp.full((1, t), 2 ** idx_bits - 1, I32)

    @pl.when(jnp.max((cnt_ge - topk).astype(F32)) > 0.0)
    def _():
        need = topk - count(lambda kj, j: kj > thr)

        def pos_step(b, jlim):
            trial = jlim + lax.shift_left(np.int32(1), np.int32(idx_bits - 1) - b)
            cnt = count(lambda kj, j: (kj == thr_eq) & (key_row + j * t < trial))
            return jnp.where(cnt < need, trial, jlim)

        j_ref[...] = lax.fori_loop(0, idx_bits, pos_step, jnp.zeros((1, t), I32))

    j_lim = j_ref[...]

    m_ref[...] = jnp.full(m_ref.shape, M_INIT, F32)
    l_ref[...] = jnp.zeros(l_ref.shape, F32)
    acc_ref[...] = jnp.zeros(acc_ref.shape, F32)

    def att_chunk(j, c):
        rows = pl.ds(pl.multiple_of(j * t, t), t)
        kj = keys_ref[rows, :]
        sel = (kj > thr) | ((kj == thr_eq) & (key_row + j * t <= j_lim))
        ckvj = ckv_ref[rows, :]
        ckvtj = ckvt_ref[j]
        bidx = jnp.minimum(i - j, 2)
        for h in range(B_HEADS):
            lg = _dot_nt(ckvj, qs_ref[:, h * B_LAT:(h + 1) * B_LAT]) + bias_ref[h, bidx]
            lg = jnp.where(sel, lg, -jnp.inf)
            m_old = m_ref[h]
            m_new = jnp.maximum(m_old, jnp.max(lg, axis=0, keepdims=True))
            alpha = jnp.exp(m_old - m_new)
            p = jnp.exp(lg - m_new)
            l_ref[h] = alpha * l_ref[h] + jnp.sum(p, axis=0, keepdims=True)
            acc_ref[h] = alpha * acc_ref[h] + _dot(ckvtj, p.astype(BF16))
            m_ref[h] = m_new
        return c

    lax.fori_loop(0, n_chunks, att_chunk, 0)

    for h in range(B_HEADS):
        o_lat_t = (acc_ref[h] * (1.0 / l_ref[h])).astype(BF16)
        o_t = _dot(wuvt_ref[h], o_lat_t)
        z = zb_ref[:, h * B_V_DIM:(h + 1) * B_V_DIM].astype(F32)
        o_ref[:, h * B_V_DIM:(h + 1) * B_V_DIM] = (o_t.T * (z / (1.0 + jnp.exp(-z)))).astype(BF16)


def _attention_b(main, ckv, ckv_t, ik2, iw_t, bias_tiles_t, w_uv_t, batch, seq, topk):
    t = ATT_TILE
    nq = seq // t
    return pl.pallas_call(
        functools.partial(_attn_b_kernel, topk=topk, seq=seq),
        grid=(batch, nq),
        in_specs=[
            pl.BlockSpec((t, B_HEADS * B_LAT), lambda b, i: (b * nq + i, _QB * LANES // (B_HEADS * B_LAT))),
            pl.BlockSpec((t, IDX_HEADS * IDX_DIM), lambda b, i: (b * nq + i, _IQ * LANES // (IDX_HEADS * IDX_DIM))),
            pl.BlockSpec((t, B_WIDTH), lambda b, i: (b * nq + i, _ZB * LANES // B_WIDTH)),
            pl.BlockSpec((seq, B_LAT), lambda b, i: (b, 0)),
            pl.BlockSpec((nq, B_LAT, t), lambda b, i: (b, 0, 0)),
            pl.BlockSpec((seq, 2 * IDX_DIM), lambda b, i: (b, 0)),
            pl.BlockSpec((IDX_HEADS, t), lambda b, i: (0, b * nq + i)),
            pl.BlockSpec((B_HEADS, 3, t, t), lambda b, i: (0, 0, 0, 0)),
            pl.BlockSpec((B_HEADS, B_V_DIM, B_LAT), lambda b, i: (0, 0, 0)),
        ],
        out_specs=pl.BlockSpec((t, B_WIDTH), lambda b, i: (b * nq + i, 0)),
        out_shape=jax.ShapeDtypeStruct((batch * seq, B_WIDTH), BF16),
        scratch_shapes=[
            pltpu.VMEM((seq, t), I32),
            pltpu.VMEM((IDX_HEADS, t, LANES), BF16),
            pltpu.VMEM((t, B_HEADS * B_LAT), BF16),
            pltpu.VMEM((B_HEADS, 1, t), F32),
            pltpu.VMEM((B_HEADS, 1, t), F32),
            pltpu.VMEM((B_HEADS, B_LAT, t), F32),
            pltpu.VMEM((1, t), I32),
        ],
        compiler_params=pltpu.CompilerParams(
            dimension_semantics=("parallel", "arbitrary"), vmem_limit_bytes=52 * 2 ** 20),
        name="attention_b",
    )(main, main, main, ckv, ckv_t, ik2, iw_t, bias_tiles_t, w_uv_t)


def _out_kernel(oa_ref, ob_ref, x_ref, w_ref, g_ref, o_ref):
    y = _dot(oa_ref[...], w_ref[0:A_WIDTH, :]) + _dot(ob_ref[...], w_ref[A_WIDTH:A_WIDTH + B_WIDTH, :])
    ms = jnp.mean(y * y, axis=-1, keepdims=True)
    o_ref[...] = x_ref[...] + y * lax.rsqrt(ms + EPS) * g_ref[...]


def _output_projection(oa, ob, x2, w_out, g):
    m, d = x2.shape
    tm = min(OUT_TM, m)
    return pl.pallas_call(
        _out_kernel,
        grid=(m // tm,),
        in_specs=[
            pl.BlockSpec((tm, A_WIDTH), lambda i: (i, 0)),
            pl.BlockSpec((tm, B_WIDTH), lambda i: (i, 0)),
            pl.BlockSpec((tm, d), lambda i: (i, 0)),
            pl.BlockSpec((A_WIDTH + B_WIDTH, d), lambda i: (0, 0)),
            pl.BlockSpec((1, d), lambda i: (0, 0)),
        ],
        out_specs=pl.BlockSpec((tm, d), lambda i: (i, 0)),
        out_shape=jax.ShapeDtypeStruct((m, d), F32),
        compiler_params=pltpu.CompilerParams(
            dimension_semantics=("parallel",), vmem_limit_bytes=52 * 2 ** 20),
        name="output_projection",
    )(oa, ob, x2, w_out, g)


def _t5_bucket(dist):
    n = jnp.maximum(dist, 0)
    max_exact = N_BUCKETS // 2
    nf = jnp.maximum(n, 1).astype(F32)
    large = max_exact + (jnp.log(nf / max_exact) / math.log(MAX_DISTANCE / max_exact)
                         * (N_BUCKETS - max_exact)).astype(I32)
    large = jnp.minimum(large, N_BUCKETS - 1)
    return jnp.where(n < max_exact, n, large)


def _bias_tiles_t(bias_tab, causal_mask):
    t = ATT_TILE
    assert t > MAX_DISTANCE
    n_heads = bias_tab.shape[1]
    by_dist = bias_tab[_t5_bucket(jnp.arange(-t, 3 * t))].T
    tiles = []
    for d in range(3):
        u = by_dist[:, d * t:d * t + 2 * t]
        flat = jnp.tile(u, (1, t + 1))[:, t:t + t * (2 * t - 1)]
        tiles.append(flat.reshape(n_heads, t, 2 * t - 1)[:, :, :t])
    tiles = jnp.stack(tiles, axis=1)
    if causal_mask:
        k = jnp.arange(t)[:, None]
        q = jnp.arange(t)[None, :]
        dist = jnp.stack([d * t + q - k for d in range(3)])
        tiles = jnp.where(dist[None] >= 0, tiles, -jnp.inf)
    return tiles


def _split_w_in(w):
    d = w.shape[0]
    sizes = (2 * A_HEADS * A_QK_DIM, 2 * A_HEADS * A_QK_DIM, A_WIDTH, A_WIDTH, B_HEADS * B_LAT, B_LAT,
             B_WIDTH, IDX_HEADS * IDX_DIM, IDX_DIM, IDX_HEADS)
    assert sum(sizes) == w.shape[1]
    offs = np.cumsum((0,) + sizes)
    qa, ka, va, za, qb, ckv, zb, iq, ik, iw = (w[:, offs[n]:offs[n + 1]] for n in range(len(sizes)))

    def by_head(m):
        return m.reshape(d, 2, A_HEADS, A_QK_DIM).transpose(0, 2, 1, 3).reshape(d, 2 * A_HEADS * A_QK_DIM)

    main = jnp.concatenate([qb, by_head(qa), by_head(ka), za, zb, iq], axis=1)
    small = jnp.concatenate([ckv, ik, ik], axis=1)
    transposed = jnp.concatenate([va, ckv, iw], axis=1).T
    assert main.shape[1] == MAIN_W and small.shape[1] == SMALL_W
    return main.astype(BF16), small.astype(BF16), transposed.astype(BF16)


def kernel(x, norm_pre_g, w_in, lambda_q1, lambda_k1, lambda_q2, lambda_k2, subln_g, kv_norm_g, idx_k_norm_g,
           w_uv, rel_bias, w_out, norm_post_g):
    batch, seq, d_model = x.shape
    t = ATT_TILE
    assert seq % t == 0 and d_model % LANES == 0
    topk = min(TOPK_MAX, seq // 4)
    row = lambda v: v.reshape(1, -1).astype(F32)

    bias_a_t = _bias_tiles_t(rel_bias[:, :A_HEADS], causal_mask=True)
    bias_b_t = _bias_tiles_t(rel_bias[:, A_HEADS:], causal_mask=False)

    x2 = x.reshape(batch * seq, d_model)
    for l in range(w_in.shape[0]):
        lam_init = 0.8 - 0.6 * math.exp(-0.3 * l)
        w_main, w_small, w_t = _split_w_in(w_in[l])
        ik_g2 = jnp.concatenate([idx_k_norm_g[l], idx_k_norm_g[l]])
        main, ckv, ik2, v_t, ckv_t, iw_t = _input_projection(
            x2, row(norm_pre_g[l]), w_main, w_small, w_t, row(kv_norm_g[l]), row(ik_g2))
        o_a = _attention_a(main, v_t, bias_a_t, row(lambda_q1[l]), row(lambda_k1[l]), row(lambda_q2[l]),
                           row(lambda_k2[l]), row(subln_g[l]), batch, seq, lam_init)
        w_uv_t = jnp.swapaxes(w_uv[l], 1, 2).astype(BF16)
        o_b = _attention_b(main, ckv, ckv_t, ik2, iw_t, bias_b_t, w_uv_t, batch, seq, topk)
        x2 = _output_projection(o_a, o_b, x2, w_out[l].astype(BF16), row(norm_post_g[l]))
    return x2.reshape(batch, seq, d_model)
```

```python
import functools
import math

import jax
import jax.numpy as jnp
import numpy as np
from jax import lax
from jax.experimental import pallas as pl
from jax.experimental.pallas import tpu as pltpu

F32 = jnp.float32
BF16 = jnp.bfloat16
I32 = jnp.int32

A_HEADS = 8
A_QK_DIM = 64
A_V_DIM = 2 * A_QK_DIM
A_WIDTH = A_HEADS * A_V_DIM
B_HEADS = 8
B_LAT = 256
B_V_DIM = 128
B_WIDTH = B_HEADS * B_V_DIM
IDX_HEADS = 16
IDX_DIM = 64
TOPK_MAX = 256
N_BUCKETS = 32
MAX_DISTANCE = 128
EPS = 1e-6

LANES = 128
ATT_TILE = 256
A_GROUP = 8
PROJ_TM = 1024
PROJ_TN = 1024
OUT_TM = 512
SMALL_W = 384
INT_MIN = np.int32(-2 ** 31)
M_INIT = -1e30

_QB, _QA, _KA, _ZA, _ZB, _IQ, _MAIN_END = 0, 16, 24, 32, 40, 48, 56
MAIN_W = _MAIN_END * LANES


def _dot(a, b):
    return jnp.dot(a, b, preferred_element_type=F32)


def _dot_nt(a, b):
    return lax.dot_general(a, b, (((1,), (1,)), ((), ())), preferred_element_type=F32)


def _rows8_sum(x):
    return x.reshape(x.shape[0] // 8, 8, x.shape[1]).sum(axis=0)


def _proj_kernel(x_ref, g_ref, w_ref, ws_ref, wt_ref, kvg_ref, kvg_col_ref, ikg_ref,
                 main_ref, ckv_ref, ik_ref, vt_ref, ckvt_ref, iwt_ref, h_ref, *, n_main):
    j = pl.program_id(1)
    t = ATT_TILE

    @pl.when(j == 0)
    def _():
        x = x_ref[...]
        ms = jnp.mean(x * x, axis=-1, keepdims=True)
        hb = (x * lax.rsqrt(ms + EPS) * g_ref[...]).astype(BF16)
        h_ref[...] = hb
        s = _dot(hb, ws_ref[...])
        ckv = s[:, :B_LAT]
        ckv_ms = jnp.mean(ckv * ckv, axis=-1, keepdims=True)
        ckv_ref[...] = (ckv * lax.rsqrt(ckv_ms + EPS) * kvg_ref[...]).astype(BF16)
        ik2 = s[:, B_LAT:B_LAT + 2 * IDX_DIM]
        ik_ms = jnp.sum(ik2 * ik2, axis=-1, keepdims=True) * (1.0 / (2 * IDX_DIM))
        ik_ref[...] = (ik2 * lax.rsqrt(ik_ms + EPS) * ikg_ref[...]).astype(BF16)

    @pl.when(j < n_main)
    def _():
        main_ref[...] = _dot(h_ref[...], w_ref[...]).astype(BF16)

    @pl.when(j == n_main)
    def _():
        h = h_ref[...]
        vt = _dot_nt(wt_ref[0:A_WIDTH, :], h).astype(BF16)
        rest = _dot_nt(wt_ref[A_WIDTH:, :], h)
        ckvt = rest[:B_LAT]
        ckvt_ms = jnp.mean(ckvt * ckvt, axis=0, keepdims=True)
        ckvt = (ckvt * lax.rsqrt(ckvt_ms + EPS) * kvg_col_ref[...]).astype(BF16)
        for c in range(vt_ref.shape[0]):
            vt_ref[c] = vt[:, c * t:(c + 1) * t]
            ckvt_ref[c] = ckvt[:, c * t:(c + 1) * t]
        iwt_ref[...] = rest[B_LAT:B_LAT + IDX_HEADS]


def _input_projection(x2, g, w_main, w_small, w_t, kv_g, ik_g2):
    m, d = x2.shape
    t = ATT_TILE
    tm = min(PROJ_TM, m)
    n_main = MAIN_W // PROJ_TN
    once = dict(pipeline_mode=pl.Buffered(1))
    return pl.pallas_call(
        functools.partial(_proj_kernel, n_main=n_main),
        grid=(m // tm, n_main + 1),
        in_specs=[
            pl.BlockSpec((tm, d), lambda i, j: (i, 0)),
            pl.BlockSpec((1, d), lambda i, j: (0, 0)),
            pl.BlockSpec((d, PROJ_TN), lambda i, j: (0, jnp.minimum(j, n_main - 1))),
            pl.BlockSpec((d, SMALL_W), lambda i, j: (0, 0), **once),
            pl.BlockSpec((A_WIDTH + B_LAT + IDX_HEADS, d), lambda i, j: (0, 0), **once),
            pl.BlockSpec((1, B_LAT), lambda i, j: (0, 0)),
            pl.BlockSpec((B_LAT, 1), lambda i, j: (0, 0)),
            pl.BlockSpec((1, 2 * IDX_DIM), lambda i, j: (0, 0)),
        ],
        out_specs=[
            pl.BlockSpec((tm, PROJ_TN), lambda i, j: (i, jnp.minimum(j, n_main - 1))),
            pl.BlockSpec((tm, B_LAT), lambda i, j: (i, 0)),
            pl.BlockSpec((tm, 2 * IDX_DIM), lambda i, j: (i, 0)),
            pl.BlockSpec((tm // t, A_WIDTH, t), lambda i, j: (i, 0, 0)),
            pl.BlockSpec((tm // t, B_LAT, t), lambda i, j: (i, 0, 0)),
            pl.BlockSpec((IDX_HEADS, tm), lambda i, j: (0, i)),
        ],
        out_shape=[
            jax.ShapeDtypeStruct((m, MAIN_W), BF16),
            jax.ShapeDtypeStruct((m, B_LAT), BF16),
            jax.ShapeDtypeStruct((m, 2 * IDX_DIM), BF16),
            jax.ShapeDtypeStruct((m // t, A_WIDTH, t), BF16),
            jax.ShapeDtypeStruct((m // t, B_LAT, t), BF16),
            jax.ShapeDtypeStruct((IDX_HEADS, m), F32),
        ],
        scratch_shapes=[pltpu.VMEM((tm, d), BF16)],
        compiler_params=pltpu.CompilerParams(
            dimension_semantics=("parallel", "arbitrary"), vmem_limit_bytes=56 * 2 ** 20),
        name="input_projection",
    )(x2, g, w_main, w_small, w_t, kv_g, kv_g.reshape(-1, 1), ik_g2)


def _attn_a_kernel(q_ref, k_ref, vt_ref, z_ref, bias_ref, lq1_ref, lk1_ref, lq2_ref, lk2_ref, g_ref,
                   o_ref, qs_ref, m_ref, l_ref, acc_ref, *, lam_init, n_blocks):
    t = ATT_TILE
    lam = (jnp.exp(jnp.sum(lq1_ref[...] * lk1_ref[...], axis=-1, keepdims=True))
           - jnp.exp(jnp.sum(lq2_ref[...] * lk2_ref[...], axis=-1, keepdims=True)) + lam_init)
    lane = lax.broadcasted_iota(I32, (t, LANES), 1)
    zero = jnp.zeros((t, LANES), BF16)
    heads = range(A_GROUP)

    def q_block(i, carry):
        rows = pl.ds(pl.multiple_of(i * t, t), t)
        for g in heads:
            q = q_ref[rows, g * LANES:(g + 1) * LANES] * (A_QK_DIM ** -0.5)
            qs_ref[g, 0:t, :] = jnp.where(lane < A_QK_DIM, q, zero)
            qs_ref[g, t:2 * t, :] = jnp.where(lane >= A_QK_DIM, q, zero)
        m_ref[...] = jnp.full(m_ref.shape, M_INIT, F32)
        l_ref[...] = jnp.zeros(l_ref.shape, F32)
        acc_ref[...] = jnp.zeros(acc_ref.shape, F32)

        def scores(j):
            keys = pl.ds(pl.multiple_of(j * t, t), t)
            return tuple(_dot_nt(k_ref[keys, g * LANES:(g + 1) * LANES], qs_ref[g]) for g in heads)

        def softmax_step(g, j, s, near):
            vtj = vt_ref[j, g * A_V_DIM:(g + 1) * A_V_DIM, :]
            for c_ in range(2):
                cols = slice(c_ * t, (c_ + 1) * t)
                sc = s[:, cols]
                if near:
                    sc = sc + bias_ref[g, i - j]
                m_old = m_ref[g, :, cols]
                m_new = jnp.maximum(m_old, jnp.max(sc, axis=0, keepdims=True))
                alpha = jnp.exp(m_old - m_new)
                p = jnp.exp(sc - m_new)
                l_ref[g, :, cols] = alpha * l_ref[g, :, cols] + jnp.sum(p, axis=0, keepdims=True)
                acc_ref[g, :, cols] = alpha * acc_ref[g, :, cols] + _dot(vtj, p.astype(BF16))
                m_ref[g, :, cols] = m_new

        def kv_chunk(near):
            def body(j, c):
                s = scores(j)
                for g in heads:
                    softmax_step(g, j, s[g], near)
                return c
            return body

        n_far = jnp.maximum(i - 1, 0)
        lax.fori_loop(0, n_far, kv_chunk(False), 0)
        lax.fori_loop(n_far, i + 1, kv_chunk(True), 0)

        for g in heads:
            inv_l = 1.0 / l_ref[g]
            a_t = (acc_ref[g, :, 0:t] * inv_l[:, 0:t]
                   - lam * (acc_ref[g, :, t:2 * t] * inv_l[:, t:2 * t]))
            y_t = a_t * lax.rsqrt(jnp.mean(a_t * a_t, axis=0, keepdims=True) + EPS)
            z = z_ref[rows, g * LANES:(g + 1) * LANES].astype(F32)
            o_ref[rows, g * LANES:(g + 1) * LANES] = (
                y_t.T * g_ref[...] * (1.0 - lam_init) * (z / (1.0 + jnp.exp(-z)))).astype(BF16)
        return carry

    lax.fori_loop(0, n_blocks, q_block, 0)


def _attention_a(main, v_t, bias_tiles_t, lq1, lk1, lq2, lk2, sub_g, batch, seq, lam_init):
    t = ATT_TILE
    nq = seq // t
    gw = A_GROUP * LANES
    vec = lambda n: pl.BlockSpec((1, n), lambda b, h: (0, 0))
    col = lambda off: pl.BlockSpec((seq, gw), lambda b, h: (b, off // A_GROUP + h))
    return pl.pallas_call(
        functools.partial(_attn_a_kernel, lam_init=lam_init, n_blocks=nq),
        grid=(batch, A_HEADS // A_GROUP),
        in_specs=[
            col(_QA), col(_KA),
            pl.BlockSpec((nq, A_GROUP * A_V_DIM, t), lambda b, h: (b, h, 0)),
            col(_ZA),
            pl.BlockSpec((A_GROUP, 2, t, t), lambda b, h: (h, 0, 0, 0)),
            vec(A_QK_DIM), vec(A_QK_DIM), vec(A_QK_DIM), vec(A_QK_DIM), vec(A_V_DIM),
        ],
        out_specs=pl.BlockSpec((seq, gw), lambda b, h: (b, h)),
        out_shape=jax.ShapeDtypeStruct((batch * seq, A_WIDTH), BF16),
        scratch_shapes=[
            pltpu.VMEM((A_GROUP, 2 * t, LANES), BF16),
            pltpu.VMEM((A_GROUP, 1, 2 * t), F32),
            pltpu.VMEM((A_GROUP, 1, 2 * t), F32),
            pltpu.VMEM((A_GROUP, A_V_DIM, 2 * t), F32),
        ],
        compiler_params=pltpu.CompilerParams(
            dimension_semantics=("parallel", "arbitrary"), vmem_limit_bytes=56 * 2 ** 20),
        name="attention_a",
    )(main, main, v_t, main, bias_tiles_t, lq1, lk1, lq2, lk2, sub_g)


def _attn_b_kernel(qb_ref, iq_ref, zb_ref, ckv_ref, ckvt_ref, ik_ref, iwt_ref, bias_ref, wuvt_ref,
                   o_ref, keys_ref, iqm_ref, qs_ref, m_ref, l_ref, acc_ref, j_ref, thr_ref, cnt_ref,
                   *, topk, seq):
    t = ATT_TILE
    i = pl.program_id(1)
    n_chunks = i + 1

    lane = lax.broadcasted_iota(I32, (t, LANES), 1)
    zero = jnp.zeros((t, LANES), BF16)
    for hp in range(IDX_HEADS // 2):
        pair = iq_ref[:, hp * LANES:(hp + 1) * LANES]
        iqm_ref[2 * hp] = jnp.where(lane < IDX_DIM, pair, zero)
        iqm_ref[2 * hp + 1] = jnp.where(lane >= IDX_DIM, pair, zero)
    qs_ref[...] = qb_ref[...] * (B_LAT ** -0.5)
    wt = iwt_ref[...] * ((IDX_HEADS ** -0.5) * (IDX_DIM ** -0.5))

    key_minus_query = (lax.broadcasted_iota(I32, (t, t), 0) - lax.broadcasted_iota(I32, (t, t), 1))
    key_row = lax.broadcasted_iota(I32, (t, t), 0)

    def idx_chunk(j, c):
        rows = pl.ds(pl.multiple_of(j * t, t), t)
        ikj = ik_ref[rows, :]
        score = jnp.zeros((t, t), F32)
        for h in range(IDX_HEADS):
            d = _dot_nt(ikj, iqm_ref[h])
            score = score + wt[h:h + 1, :] * jnp.maximum(d, 0.0)
        bits = lax.bitcast_convert_type(score, I32)
        key = bits ^ ((bits >> 31) & np.int32(0x7FFFFFFF))
        causal = key_minus_query <= (i - j) * t
        keys_ref[rows, :] = jnp.where(causal, key, INT_MIN)
        return c

    lax.fori_loop(0, n_chunks, idx_chunk, 0)

    def count(pred_fn):
        def body(j, c8):
            kj = keys_ref[pl.ds(pl.multiple_of(j * t, t), t), :]
            return c8 + _rows8_sum(pred_fn(kj, j).astype(I32))
        c8 = lax.fori_loop(0, n_chunks, body, jnp.zeros((8, t), I32))
        return jnp.sum(c8, axis=0, keepdims=True)

    def bisect(n_static):
        def bit_step(b, carry):
            cu, cnt_at = carry
            trial_u = cu | lax.shift_left(np.int32(1), np.int32(31) - b)
            trial = trial_u ^ INT_MIN
            c8 = jnp.zeros((8, t), I32)
            for j in range(n_static):
                c8 = c8 + _rows8_sum((keys_ref[j * t:(j + 1) * t, :] >= trial).astype(I32))
            cnt = jnp.sum(c8, axis=0, keepdims=True)
            ok = cnt >= topk
            return jnp.where(ok, trial_u, cu), jnp.where(ok, cnt, cnt_at)

        cu, cnt = lax.fori_loop(0, 32, bit_step, (jnp.zeros((1, t), I32), jnp.zeros((1, t), I32)))
        thr_ref[...] = cu ^ INT_MIN
        cnt_ref[...] = cnt

    for n_static in range(1, seq // t + 1):
        pl.when(i == n_static - 1)(functools.partial(bisect, n_static))

    thr = thr_ref[...]
    cnt_ge = cnt_ref[...]
    thr_eq = jnp.maximum(thr, INT_MIN + np.int32(1))

    idx_bits = max(1, (seq - 1).bit_length())
    j_ref[...] = jnp.full((1, t), 2 ** idx_bits - 1, I32)

    @pl.when(jnp.max((cnt_ge - topk).astype(F32)) > 0.0)
    def _():
        need = topk - count(lambda kj, j: kj > thr)

        def pos_step(b, jlim):
            trial = jlim + lax.shift_left(np.int32(1), np.int32(idx_bits - 1) - b)
            cnt = count(lambda kj, j: (kj == thr_eq) & (key_row + j * t < trial))
            return jnp.where(cnt < need, trial, jlim)

        j_ref[...] = lax.fori_loop(0, idx_bits, pos_step, jnp.zeros((1, t), I32))

    j_lim = j_ref[...]

    m_ref[...] = jnp.full(m_ref.shape, M_INIT, F32)
    l_ref[...] = jnp.zeros(l_ref.shape, F32)
    acc_ref[...] = jnp.zeros(acc_ref.shape, F32)

    def att_chunk(near):
        def body(j, c):
            rows = pl.ds(pl.multiple_of(j * t, t), t)
            kj = keys_ref[rows, :]
            sel = (kj > thr) | ((kj == thr_eq) & (key_row + j * t <= j_lim))
            ckvj = ckv_ref[rows, :]
            ckvtj = ckvt_ref[j]
            for h in range(B_HEADS):
                lg = _dot_nt(ckvj, qs_ref[:, h * B_LAT:(h + 1) * B_LAT])
                if near:
                    lg = lg + bias_ref[h, i - j]
                lg = jnp.where(sel, lg, -jnp.inf)
                m_old = m_ref[h]
                m_new = jnp.maximum(m_old, jnp.max(lg, axis=0, keepdims=True))
                alpha = jnp.exp(m_old - m_new)
                p = jnp.exp(lg - m_new)
                l_ref[h] = alpha * l_ref[h] + jnp.sum(p, axis=0, keepdims=True)
                acc_ref[h] = alpha * acc_ref[h] + _dot(ckvtj, p.astype(BF16))
                m_ref[h] = m_new
            return c
        return body

    n_far = jnp.maximum(i - 1, 0)
    lax.fori_loop(0, n_far, att_chunk(False), 0)
    lax.fori_loop(n_far, n_chunks, att_chunk(True), 0)

    for h in range(B_HEADS):
        o_lat_t = (acc_ref[h] * (1.0 / l_ref[h])).astype(BF16)
        o_t = _dot(wuvt_ref[h], o_lat_t)
        z = zb_ref[:, h * B_V_DIM:(h + 1) * B_V_DIM].astype(F32)
        o_ref[:, h * B_V_DIM:(h + 1) * B_V_DIM] = (o_t.T * (z / (1.0 + jnp.exp(-z)))).astype(BF16)


def _attention_b(main, ckv, ckv_t, ik2, iw_t, bias_tiles_t, w_uv_t, batch, seq, topk):
    t = ATT_TILE
    nq = seq // t
    return pl.pallas_call(
        functools.partial(_attn_b_kernel, topk=topk, seq=seq),
        grid=(batch, nq),
        in_specs=[
            pl.BlockSpec((t, B_HEADS * B_LAT), lambda b, i: (b * nq + i, _QB * LANES // (B_HEADS * B_LAT))),
            pl.BlockSpec((t, IDX_HEADS * IDX_DIM), lambda b, i: (b * nq + i, _IQ * LANES // (IDX_HEADS * IDX_DIM))),
            pl.BlockSpec((t, B_WIDTH), lambda b, i: (b * nq + i, _ZB * LANES // B_WIDTH)),
            pl.BlockSpec((seq, B_LAT), lambda b, i: (b, 0)),
            pl.BlockSpec((nq, B_LAT, t), lambda b, i: (b, 0, 0)),
            pl.BlockSpec((seq, 2 * IDX_DIM), lambda b, i: (b, 0)),
            pl.BlockSpec((IDX_HEADS, t), lambda b, i: (0, b * nq + i)),
            pl.BlockSpec((B_HEADS, 2, t, t), lambda b, i: (0, 0, 0, 0)),
            pl.BlockSpec((B_HEADS, B_V_DIM, B_LAT), lambda b, i: (0, 0, 0)),
        ],
        out_specs=pl.BlockSpec((t, B_WIDTH), lambda b, i: (b * nq + i, 0)),
        out_shape=jax.ShapeDtypeStruct((batch * seq, B_WIDTH), BF16),
        scratch_shapes=[
            pltpu.VMEM((seq, t), I32),
            pltpu.VMEM((IDX_HEADS, t, LANES), BF16),
            pltpu.VMEM((t, B_HEADS * B_LAT), BF16),
            pltpu.VMEM((B_HEADS, 1, t), F32),
            pltpu.VMEM((B_HEADS, 1, t), F32),
            pltpu.VMEM((B_HEADS, B_LAT, t), F32),
            pltpu.VMEM((1, t), I32),
            pltpu.VMEM((1, t), I32),
            pltpu.VMEM((1, t), I32),
        ],
        compiler_params=pltpu.CompilerParams(
            dimension_semantics=("parallel", "arbitrary"), vmem_limit_bytes=52 * 2 ** 20),
        name="attention_b",
    )(main, main, main, ckv, ckv_t, ik2, iw_t, bias_tiles_t, w_uv_t)


def _out_kernel(oa_ref, ob_ref, x_ref, w_ref, g_ref, o_ref):
    y = _dot(oa_ref[...], w_ref[0:A_WIDTH, :]) + _dot(ob_ref[...], w_ref[A_WIDTH:A_WIDTH + B_WIDTH, :])
    ms = jnp.mean(y * y, axis=-1, keepdims=True)
    o_ref[...] = x_ref[...] + y * lax.rsqrt(ms + EPS) * g_ref[...]


def _output_projection(oa, ob, x2, w_out, g):
    m, d = x2.shape
    tm = min(OUT_TM, m)
    return pl.pallas_call(
        _out_kernel,
        grid=(m // tm,),
        in_specs=[
            pl.BlockSpec((tm, A_WIDTH), lambda i: (i, 0)),
            pl.BlockSpec((tm, B_WIDTH), lambda i: (i, 0)),
            pl.BlockSpec((tm, d), lambda i: (i, 0)),
            pl.BlockSpec((A_WIDTH + B_WIDTH, d), lambda i: (0, 0)),
            pl.BlockSpec((1, d), lambda i: (0, 0)),
        ],
        out_specs=pl.BlockSpec((tm, d), lambda i: (i, 0)),
        out_shape=jax.ShapeDtypeStruct((m, d), F32),
        compiler_params=pltpu.CompilerParams(
            dimension_semantics=("parallel",), vmem_limit_bytes=52 * 2 ** 20),
        name="output_projection",
    )(oa, ob, x2, w_out, g)


def _t5_bucket(dist):
    n = jnp.maximum(dist, 0)
    max_exact = N_BUCKETS // 2
    nf = jnp.maximum(n, 1).astype(F32)
    large = max_exact + (jnp.log(nf / max_exact) / math.log(MAX_DISTANCE / max_exact)
                         * (N_BUCKETS - max_exact)).astype(I32)
    large = jnp.minimum(large, N_BUCKETS - 1)
    return jnp.where(n < max_exact, n, large)


def _bias_tiles_t(bias_tab):
    t = ATT_TILE
    assert t > MAX_DISTANCE
    n_heads = bias_tab.shape[1]
    by_dist = bias_tab[_t5_bucket(jnp.arange(-t, 3 * t))].T
    tiles = []
    for d in range(3):
        u = by_dist[:, d * t:d * t + 2 * t]
        flat = jnp.tile(u, (1, t + 1))[:, t:t + t * (2 * t - 1)]
        tiles.append(flat.reshape(n_heads, t, 2 * t - 1)[:, :, :t])
    return jnp.stack(tiles, axis=1)


def _near_bias_tiles_t(bias_tab, causal_mask):
    t = ATT_TILE
    tiles = _bias_tiles_t(bias_tab)
    near = tiles[:, :2] - tiles[:, 2:3, :1, :1]
    if causal_mask:
        k = jnp.arange(t)[:, None]
        q = jnp.arange(t)[None, :]
        dist = jnp.stack([d * t + q - k for d in range(2)])
        near = jnp.where(dist[None] >= 0, near, -jnp.inf)
    return near


def _split_w_in(w):
    d = w.shape[0]
    sizes = (2 * A_HEADS * A_QK_DIM, 2 * A_HEADS * A_QK_DIM, A_WIDTH, A_WIDTH, B_HEADS * B_LAT, B_LAT,
             B_WIDTH, IDX_HEADS * IDX_DIM, IDX_DIM, IDX_HEADS)
    assert sum(sizes) == w.shape[1]
    offs = np.cumsum((0,) + sizes)
    qa, ka, va, za, qb, ckv, zb, iq, ik, iw = (w[:, offs[n]:offs[n + 1]] for n in range(len(sizes)))

    def by_head(m):
        return m.reshape(d, 2, A_HEADS, A_QK_DIM).transpose(0, 2, 1, 3).reshape(d, 2 * A_HEADS * A_QK_DIM)

    main = jnp.concatenate([qb, by_head(qa), by_head(ka), za, zb, iq], axis=1)
    small = jnp.concatenate([ckv, ik, ik], axis=1)
    transposed = jnp.concatenate([va, ckv, iw], axis=1).T
    assert main.shape[1] == MAIN_W and small.shape[1] == SMALL_W
    return main.astype(BF16), small.astype(BF16), transposed.astype(BF16)


def kernel(x, norm_pre_g, w_in, lambda_q1, lambda_k1, lambda_q2, lambda_k2, subln_g, kv_norm_g, idx_k_norm_g,
           w_uv, rel_bias, w_out, norm_post_g):
    batch, seq, d_model = x.shape
    t = ATT_TILE
    assert seq % t == 0 and d_model % LANES == 0
    topk = min(TOPK_MAX, seq // 4)
    row = lambda v: v.reshape(1, -1).astype(F32)

    bias_a_t = _near_bias_tiles_t(rel_bias[:, :A_HEADS], causal_mask=True)
    bias_b_t = _near_bias_tiles_t(rel_bias[:, A_HEADS:], causal_mask=False)

    x2 = x.reshape(batch * seq, d_model)
    for l in range(w_in.shape[0]):
        lam_init = 0.8 - 0.6 * math.exp(-0.3 * l)
        w_main, w_small, w_t = _split_w_in(w_in[l])
        ik_g2 = jnp.concatenate([idx_k_norm_g[l], idx_k_norm_g[l]])
        main, ckv, ik2, v_t, ckv_t, iw_t = _input_projection(
            x2, row(norm_pre_g[l]), w_main, w_small, w_t, row(kv_norm_g[l]), row(ik_g2))
        o_a = _attention_a(main, v_t, bias_a_t, row(lambda_q1[l]), row(lambda_k1[l]), row(lambda_q2[l]),
                           row(lambda_k2[l]), row(subln_g[l]), batch, seq, lam_init)
        w_uv_t = jnp.swapaxes(w_uv[l], 1, 2).astype(BF16)
        o_b = _attention_b(main, ckv, ckv_t, ik2, iw_t, bias_b_t, w_uv_t, batch, seq, topk)
        x2 = _output_projection(o_a, o_b, x2, w_out[l].astype(BF16), row(norm_post_g[l]))
    return x2.reshape(batch, seq, d_model)
```

```python
import functools
import math

import jax
import jax.numpy as jnp
import numpy as np
from jax import lax
from jax.experimental import pallas as pl
from jax.experimental.pallas import tpu as pltpu

F32 = jnp.float32
BF16 = jnp.bfloat16
I32 = jnp.int32

A_HEADS = 8
A_QK_DIM = 64
A_V_DIM = 2 * A_QK_DIM
A_WIDTH = A_HEADS * A_V_DIM
B_HEADS = 8
B_LAT = 256
B_V_DIM = 128
B_WIDTH = B_HEADS * B_V_DIM
IDX_HEADS = 16
IDX_DIM = 64
TOPK_MAX = 256
N_BUCKETS = 32
MAX_DISTANCE = 128
EPS = 1e-6

LANES = 128
ATT_TILE = 256
A_GROUP = 8
PROJ_TM = 1024
PROJ_TN = 1024
OUT_TM = 512
SMALL_W = 384
INT_MIN = np.int32(-2 ** 31)
M_INIT = -1e30

_QB, _QA, _KA, _ZA, _ZB, _IQ, _MAIN_END = 0, 16, 24, 32, 40, 48, 56
MAIN_W = _MAIN_END * LANES


def _dot(a, b):
    return jnp.dot(a, b, preferred_element_type=F32)


def _dot_nt(a, b):
    return lax.dot_general(a, b, (((1,), (1,)), ((), ())), preferred_element_type=F32)


def _causal_tile_runs(i, run):
    far, near = False, True
    n_far = jnp.maximum(i - 1, 0)

    def far_pair(jj, c):
        run(2 * jj, (far, far))
        return c

    lax.fori_loop(0, lax.shift_right_logical(n_far, 1), far_pair, 0)
    pl.when(i == 0)(lambda: run(0, (near,)))
    pl.when((i >= 1) & (n_far % 2 == 0))(lambda: run(i - 1, (near, near)))
    pl.when(n_far % 2 == 1)(lambda: run(i - 2, (far, near, near)))


def _rows8_sum(x):
    return x.reshape(x.shape[0] // 8, 8, x.shape[1]).sum(axis=0)


def _proj_kernel(x_ref, g_ref, w_ref, ws_ref, wt_ref, kvg_ref, kvg_col_ref, ikg_ref,
                 main_ref, ckv_ref, ik_ref, vt_ref, ckvt_ref, iwt_ref, h_ref, *, n_main):
    j = pl.program_id(1)
    t = ATT_TILE

    @pl.when(j == 0)
    def _():
        x = x_ref[...]
        ms = jnp.mean(x * x, axis=-1, keepdims=True)
        hb = (x * lax.rsqrt(ms + EPS) * g_ref[...]).astype(BF16)
        h_ref[...] = hb
        s = _dot(hb, ws_ref[...])
        ckv = s[:, :B_LAT]
        ckv_ms = jnp.mean(ckv * ckv, axis=-1, keepdims=True)
        ckv_ref[...] = (ckv * lax.rsqrt(ckv_ms + EPS) * kvg_ref[...]).astype(BF16)
        ik2 = s[:, B_LAT:B_LAT + 2 * IDX_DIM]
        ik_ms = jnp.sum(ik2 * ik2, axis=-1, keepdims=True) * (1.0 / (2 * IDX_DIM))
        ik_ref[...] = (ik2 * lax.rsqrt(ik_ms + EPS) * ikg_ref[...]).astype(BF16)

    @pl.when(j < n_main)
    def _():
        main_ref[...] = _dot(h_ref[...], w_ref[...]).astype(BF16)

    @pl.when(j == n_main)
    def _():
        h = h_ref[...]
        vt = _dot_nt(wt_ref[0:A_WIDTH, :], h).astype(BF16)
        rest = _dot_nt(wt_ref[A_WIDTH:, :], h)
        ckvt = rest[:B_LAT]
        ckvt_ms = jnp.mean(ckvt * ckvt, axis=0, keepdims=True)
        ckvt = (ckvt * lax.rsqrt(ckvt_ms + EPS) * kvg_col_ref[...]).astype(BF16)
        for c in range(vt_ref.shape[0]):
            vt_ref[c] = vt[:, c * t:(c + 1) * t]
            ckvt_ref[c] = ckvt[:, c * t:(c + 1) * t]
        iwt_ref[...] = rest[B_LAT:B_LAT + IDX_HEADS]


def _input_projection(x2, g, w_main, w_small, w_t, kv_g, ik_g2):
    m, d = x2.shape
    t = ATT_TILE
    tm = min(PROJ_TM, m)
    n_main = MAIN_W // PROJ_TN
    once = dict(pipeline_mode=pl.Buffered(1))
    return pl.pallas_call(
        functools.partial(_proj_kernel, n_main=n_main),
        grid=(m // tm, n_main + 1),
        in_specs=[
            pl.BlockSpec((tm, d), lambda i, j: (i, 0)),
            pl.BlockSpec((1, d), lambda i, j: (0, 0)),
            pl.BlockSpec((d, PROJ_TN), lambda i, j: (0, jnp.minimum(j, n_main - 1))),
            pl.BlockSpec((d, SMALL_W), lambda i, j: (0, 0), **once),
            pl.BlockSpec((A_WIDTH + B_LAT + IDX_HEADS, d), lambda i, j: (0, 0), **once),
            pl.BlockSpec((1, B_LAT), lambda i, j: (0, 0)),
            pl.BlockSpec((B_LAT, 1), lambda i, j: (0, 0)),
            pl.BlockSpec((1, 2 * IDX_DIM), lambda i, j: (0, 0)),
        ],
        out_specs=[
            pl.BlockSpec((tm, PROJ_TN), lambda i, j: (i, jnp.minimum(j, n_main - 1))),
            pl.BlockSpec((tm, B_LAT), lambda i, j: (i, 0)),
            pl.BlockSpec((tm, 2 * IDX_DIM), lambda i, j: (i, 0)),
            pl.BlockSpec((tm // t, A_WIDTH, t), lambda i, j: (i, 0, 0)),
            pl.BlockSpec((tm // t, B_LAT, t), lambda i, j: (i, 0, 0)),
            pl.BlockSpec((IDX_HEADS, tm), lambda i, j: (0, i)),
        ],
        out_shape=[
            jax.ShapeDtypeStruct((m, MAIN_W), BF16),
            jax.ShapeDtypeStruct((m, B_LAT), BF16),
            jax.ShapeDtypeStruct((m, 2 * IDX_DIM), BF16),
            jax.ShapeDtypeStruct((m // t, A_WIDTH, t), BF16),
            jax.ShapeDtypeStruct((m // t, B_LAT, t), BF16),
            jax.ShapeDtypeStruct((IDX_HEADS, m), F32),
        ],
        scratch_shapes=[pltpu.VMEM((tm, d), BF16)],
        compiler_params=pltpu.CompilerParams(
            dimension_semantics=("parallel", "arbitrary"), vmem_limit_bytes=56 * 2 ** 20),
        name="input_projection",
    )(x2, g, w_main, w_small, w_t, kv_g, kv_g.reshape(-1, 1), ik_g2)


def _attn_a_kernel(q_ref, k_ref, vt_ref, z_ref, bias_ref, lq1_ref, lk1_ref, lq2_ref, lk2_ref, g_ref,
                   o_ref, qs_ref, m_ref, l_ref, acc_ref, *, lam_init, n_blocks):
    t = ATT_TILE
    lam = (jnp.exp(jnp.sum(lq1_ref[...] * lk1_ref[...], axis=-1, keepdims=True))
           - jnp.exp(jnp.sum(lq2_ref[...] * lk2_ref[...], axis=-1, keepdims=True)) + lam_init)
    lane = lax.broadcasted_iota(I32, (t, LANES), 1)
    zero = jnp.zeros((t, LANES), BF16)
    heads = range(A_GROUP)

    def q_block(i, carry):
        rows = pl.ds(pl.multiple_of(i * t, t), t)
        for g in heads:
            q = q_ref[rows, g * LANES:(g + 1) * LANES] * (A_QK_DIM ** -0.5)
            qs_ref[g, 0:t, :] = jnp.where(lane < A_QK_DIM, q, zero)
            qs_ref[g, t:2 * t, :] = jnp.where(lane >= A_QK_DIM, q, zero)
        m_ref[...] = jnp.full(m_ref.shape, M_INIT, F32)
        l_ref[...] = jnp.zeros(l_ref.shape, F32)
        acc_ref[...] = jnp.zeros(acc_ref.shape, F32)

        def scores(g, j):
            keys = pl.ds(pl.multiple_of(j * t, t), t)
            return _dot_nt(k_ref[keys, g * LANES:(g + 1) * LANES], qs_ref[g])

        def softmax_step(g, j, s, near):
            vtj = vt_ref[j, g * A_V_DIM:(g + 1) * A_V_DIM, :]
            for c_ in range(2):
                cols = slice(c_ * t, (c_ + 1) * t)
                sc = s[:, cols]
                if near:
                    sc = sc + bias_ref[g, i - j]
                m_old = m_ref[g, :, cols]
                m_new = jnp.maximum(m_old, jnp.max(sc, axis=0, keepdims=True))
                alpha = jnp.exp(m_old - m_new)
                p = jnp.exp(sc - m_new)
                l_ref[g, :, cols] = alpha * l_ref[g, :, cols] + jnp.sum(p, axis=0, keepdims=True)
                acc_ref[g, :, cols] = alpha * acc_ref[g, :, cols] + _dot(vtj, p.astype(BF16))
                m_ref[g, :, cols] = m_new

        def kv_tiles(j0, kinds):
            for c_, near in enumerate(kinds):
                s = [scores(g, j0 + c_) for g in heads]
                for g in heads:
                    softmax_step(g, j0 + c_, s[g], near)

        _causal_tile_runs(i, kv_tiles)

        for g in heads:
            inv_l = 1.0 / l_ref[g]
            a_t = (acc_ref[g, :, 0:t] * inv_l[:, 0:t]
                   - lam * (acc_ref[g, :, t:2 * t] * inv_l[:, t:2 * t]))
            y_t = a_t * lax.rsqrt(jnp.mean(a_t * a_t, axis=0, keepdims=True) + EPS)
            z = z_ref[rows, g * LANES:(g + 1) * LANES].astype(F32)
            o_ref[rows, g * LANES:(g + 1) * LANES] = (
                y_t.T * g_ref[...] * (1.0 - lam_init) * (z / (1.0 + jnp.exp(-z)))).astype(BF16)
        return carry

    lax.fori_loop(0, n_blocks, q_block, 0)


def _attention_a(main, v_t, bias_tiles_t, lq1, lk1, lq2, lk2, sub_g, batch, seq, lam_init):
    t = ATT_TILE
    nq = seq // t
    gw = A_GROUP * LANES
    vec = lambda n: pl.BlockSpec((1, n), lambda b, h: (0, 0))
    col = lambda off: pl.BlockSpec((seq, gw), lambda b, h: (b, off // A_GROUP + h))
    return pl.pallas_call(
        functools.partial(_attn_a_kernel, lam_init=lam_init, n_blocks=nq),
        grid=(batch, A_HEADS // A_GROUP),
        in_specs=[
            col(_QA), col(_KA),
            pl.BlockSpec((nq, A_GROUP * A_V_DIM, t), lambda b, h: (b, h, 0)),
            col(_ZA),
            pl.BlockSpec((A_GROUP, 2, t, t), lambda b, h: (h, 0, 0, 0)),
            vec(A_QK_DIM), vec(A_QK_DIM), vec(A_QK_DIM), vec(A_QK_DIM), vec(A_V_DIM),
        ],
        out_specs=pl.BlockSpec((seq, gw), lambda b, h: (b, h)),
        out_shape=jax.ShapeDtypeStruct((batch * seq, A_WIDTH), BF16),
        scratch_shapes=[
            pltpu.VMEM((A_GROUP, 2 * t, LANES), BF16),
            pltpu.VMEM((A_GROUP, 1, 2 * t), F32),
            pltpu.VMEM((A_GROUP, 1, 2 * t), F32),
            pltpu.VMEM((A_GROUP, A_V_DIM, 2 * t), F32),
        ],
        compiler_params=pltpu.CompilerParams(
            dimension_semantics=("parallel", "arbitrary"), vmem_limit_bytes=56 * 2 ** 20),
        name="attention_a",
    )(main, main, v_t, main, bias_tiles_t, lq1, lk1, lq2, lk2, sub_g)


def _attn_b_kernel(qb_ref, iq_ref, zb_ref, ckv_ref, ckvt_ref, ik_ref, iwt_ref, bias_ref, wuvt_ref,
                   o_ref, keys_ref, iqm_ref, qs_ref, m_ref, l_ref, acc_ref, j_ref, thr_ref, cnt_ref,
                   *, topk, seq):
    t = ATT_TILE
    i = pl.program_id(1)
    n_chunks = i + 1

    lane = lax.broadcasted_iota(I32, (t, LANES), 1)
    zero = jnp.zeros((t, LANES), BF16)
    for hp in range(IDX_HEADS // 2):
        pair = iq_ref[:, hp * LANES:(hp + 1) * LANES]
        iqm_ref[2 * hp] = jnp.where(lane < IDX_DIM, pair, zero)
        iqm_ref[2 * hp + 1] = jnp.where(lane >= IDX_DIM, pair, zero)
    qs_ref[...] = qb_ref[...] * (B_LAT ** -0.5)
    wt = iwt_ref[...] * ((IDX_HEADS ** -0.5) * (IDX_DIM ** -0.5))

    key_minus_query = (lax.broadcasted_iota(I32, (t, t), 0) - lax.broadcasted_iota(I32, (t, t), 1))
    key_row = lax.broadcasted_iota(I32, (t, t), 0)

    def idx_chunk(j, c):
        rows = pl.ds(pl.multiple_of(j * t, t), t)
        ikj = ik_ref[rows, :]
        score = jnp.zeros((t, t), F32)
        for h in range(IDX_HEADS):
            d = _dot_nt(ikj, iqm_ref[h])
            score = score + wt[h:h + 1, :] * jnp.maximum(d, 0.0)
        bits = lax.bitcast_convert_type(score, I32)
        key = bits ^ ((bits >> 31) & np.int32(0x7FFFFFFF))
        causal = key_minus_query <= (i - j) * t
        keys_ref[rows, :] = jnp.where(causal, key, INT_MIN)
        return c

    lax.fori_loop(0, n_chunks, idx_chunk, 0)

    def count(pred_fn):
        def body(j, c8):
            kj = keys_ref[pl.ds(pl.multiple_of(j * t, t), t), :]
            return c8 + _rows8_sum(pred_fn(kj, j).astype(I32))
        c8 = lax.fori_loop(0, n_chunks, body, jnp.zeros((8, t), I32))
        return jnp.sum(c8, axis=0, keepdims=True)

    def bisect(n_static):
        def bit_step(b, carry):
            cu, cnt_at = carry
            trial_u = cu | lax.shift_left(np.int32(1), np.int32(31) - b)
            trial = trial_u ^ INT_MIN
            c8 = jnp.zeros((8, t), I32)
            for j in range(n_static):
                c8 = c8 + _rows8_sum((keys_ref[j * t:(j + 1) * t, :] >= trial).astype(I32))
            cnt = jnp.sum(c8, axis=0, keepdims=True)
            ok = cnt >= topk
            return jnp.where(ok, trial_u, cu), jnp.where(ok, cnt, cnt_at)

        cu, cnt = lax.fori_loop(0, 32, bit_step, (jnp.zeros((1, t), I32), jnp.zeros((1, t), I32)))
        thr_ref[...] = cu ^ INT_MIN
        cnt_ref[...] = cnt

    for n_static in range(1, seq // t + 1):
        pl.when(i == n_static - 1)(functools.partial(bisect, n_static))

    thr = thr_ref[...]
    cnt_ge = cnt_ref[...]
    thr_eq = jnp.maximum(thr, INT_MIN + np.int32(1))

    idx_bits = max(1, (seq - 1).bit_length())
    j_ref[...] = jnp.full((1, t), 2 ** idx_bits - 1, I32)

    @pl.when(jnp.max((cnt_ge - topk).astype(F32)) > 0.0)
    def _():
        need = topk - count(lambda kj, j: kj > thr)

        def pos_step(b, jlim):
            trial = jlim + lax.shift_left(np.int32(1), np.int32(idx_bits - 1) - b)
            cnt = count(lambda kj, j: (kj == thr_eq) & (key_row + j * t < trial))
            return jnp.where(cnt < need, trial, jlim)

        j_ref[...] = lax.fori_loop(0, idx_bits, pos_step, jnp.zeros((1, t), I32))

    j_lim = j_ref[...]

    m_ref[...] = jnp.full(m_ref.shape, M_INIT, F32)
    l_ref[...] = jnp.zeros(l_ref.shape, F32)
    acc_ref[...] = jnp.zeros(acc_ref.shape, F32)

    def att_tiles(j0, kinds):
        for c_, near in enumerate(kinds):
            j = j0 + c_
            rows = pl.ds(pl.multiple_of(j * t, t), t)
            kj = keys_ref[rows, :]
            sel = (kj > thr) | ((kj == thr_eq) & (key_row + j * t <= j_lim))
            sel_bias = jnp.where(sel, 0.0, -jnp.inf)
            ckvj = ckv_ref[rows, :]
            ckvtj = ckvt_ref[j]
            raw = [_dot_nt(ckvj, qs_ref[:, h * B_LAT:(h + 1) * B_LAT]) for h in range(B_HEADS)]
            ps, alphas = [], []
            for h in range(B_HEADS):
                lg = raw[h] + sel_bias
                if near:
                    lg = lg + bias_ref[h, i - j]
                m_old = m_ref[h]
                m_new = jnp.maximum(m_old, jnp.max(lg, axis=0, keepdims=True))
                alpha = jnp.exp(m_old - m_new)
                p = jnp.exp(lg - m_new)
                l_ref[h] = alpha * l_ref[h] + jnp.sum(p, axis=0, keepdims=True)
                m_ref[h] = m_new
                ps.append(p.astype(BF16))
                alphas.append(alpha)
            pvs = [_dot(ckvtj, ps[h]) for h in range(B_HEADS)]
            for h in range(B_HEADS):
                acc_ref[h] = alphas[h] * acc_ref[h] + pvs[h]

    _causal_tile_runs(i, att_tiles)

    for h in range(B_HEADS):
        o_lat_t = (acc_ref[h] * (1.0 / l_ref[h])).astype(BF16)
        o_t = _dot(wuvt_ref[h], o_lat_t)
        z = zb_ref[:, h * B_V_DIM:(h + 1) * B_V_DIM].astype(F32)
        o_ref[:, h * B_V_DIM:(h + 1) * B_V_DIM] = (o_t.T * (z / (1.0 + jnp.exp(-z)))).astype(BF16)


def _attention_b(main, ckv, ckv_t, ik2, iw_t, bias_tiles_t, w_uv_t, batch, seq, topk):
    t = ATT_TILE
    nq = seq // t
    return pl.pallas_call(
        functools.partial(_attn_b_kernel, topk=topk, seq=seq),
        grid=(batch, nq),
        in_specs=[
            pl.BlockSpec((t, B_HEADS * B_LAT), lambda b, i: (b * nq + i, _QB * LANES // (B_HEADS * B_LAT))),
            pl.BlockSpec((t, IDX_HEADS * IDX_DIM), lambda b, i: (b * nq + i, _IQ * LANES // (IDX_HEADS * IDX_DIM))),
            pl.BlockSpec((t, B_WIDTH), lambda b, i: (b * nq + i, _ZB * LANES // B_WIDTH)),
            pl.BlockSpec((seq, B_LAT), lambda b, i: (b, 0)),
            pl.BlockSpec((nq, B_LAT, t), lambda b, i: (b, 0, 0)),
            pl.BlockSpec((seq, 2 * IDX_DIM), lambda b, i: (b, 0)),
            pl.BlockSpec((IDX_HEADS, t), lambda b, i: (0, b * nq + i)),
            pl.BlockSpec((B_HEADS, 2, t, t), lambda b, i: (0, 0, 0, 0)),
            pl.BlockSpec((B_HEADS, B_V_DIM, B_LAT), lambda b, i: (0, 0, 0)),
        ],
        out_specs=pl.BlockSpec((t, B_WIDTH), lambda b, i: (b * nq + i, 0)),
        out_shape=jax.ShapeDtypeStruct((batch * seq, B_WIDTH), BF16),
        scratch_shapes=[
            pltpu.VMEM((seq, t), I32),
            pltpu.VMEM((IDX_HEADS, t, LANES), BF16),
            pltpu.VMEM((t, B_HEADS * B_LAT), BF16),
            pltpu.VMEM((B_HEADS, 1, t), F32),
            pltpu.VMEM((B_HEADS, 1, t), F32),
            pltpu.VMEM((B_HEADS, B_LAT, t), F32),
            pltpu.VMEM((1, t), I32),
            pltpu.VMEM((1, t), I32),
            pltpu.VMEM((1, t), I32),
        ],
        compiler_params=pltpu.CompilerParams(
            dimension_semantics=("parallel", "arbitrary"), vmem_limit_bytes=52 * 2 ** 20),
        name="attention_b",
    )(main, main, main, ckv, ckv_t, ik2, iw_t, bias_tiles_t, w_uv_t)


def _out_kernel(oa_ref, ob_ref, x_ref, w_ref, g_ref, o_ref):
    y = _dot(oa_ref[...], w_ref[0:A_WIDTH, :]) + _dot(ob_ref[...], w_ref[A_WIDTH:A_WIDTH + B_WIDTH, :])
    ms = jnp.mean(y * y, axis=-1, keepdims=True)
    o_ref[...] = x_ref[...] + y * lax.rsqrt(ms + EPS) * g_ref[...]


def _output_projection(oa, ob, x2, w_out, g):
    m, d = x2.shape
    tm = min(OUT_TM, m)
    return pl.pallas_call(
        _out_kernel,
        grid=(m // tm,),
        in_specs=[
            pl.BlockSpec((tm, A_WIDTH), lambda i: (i, 0)),
            pl.BlockSpec((tm, B_WIDTH), lambda i: (i, 0)),
            pl.BlockSpec((tm, d), lambda i: (i, 0)),
            pl.BlockSpec((A_WIDTH + B_WIDTH, d), lambda i: (0, 0)),
            pl.BlockSpec((1, d), lambda i: (0, 0)),
        ],
        out_specs=pl.BlockSpec((tm, d), lambda i: (i, 0)),
        out_shape=jax.ShapeDtypeStruct((m, d), F32),
        compiler_params=pltpu.CompilerParams(
            dimension_semantics=("parallel",), vmem_limit_bytes=52 * 2 ** 20),
        name="output_projection",
    )(oa, ob, x2, w_out, g)


def _t5_bucket(dist):
    n = jnp.maximum(dist, 0)
    max_exact = N_BUCKETS // 2
    nf = jnp.maximum(n, 1).astype(F32)
    large = max_exact + (jnp.log(nf / max_exact) / math.log(MAX_DISTANCE / max_exact)
                         * (N_BUCKETS - max_exact)).astype(I32)
    large = jnp.minimum(large, N_BUCKETS - 1)
    return jnp.where(n < max_exact, n, large)


def _bias_tiles_t(bias_tab):
    t = ATT_TILE
    assert t > MAX_DISTANCE
    n_heads = bias_tab.shape[1]
    by_dist = bias_tab[_t5_bucket(jnp.arange(-t, 3 * t))].T
    tiles = []
    for d in range(3):
        u = by_dist[:, d * t:d * t + 2 * t]
        flat = jnp.tile(u, (1, t + 1))[:, t:t + t * (2 * t - 1)]
        tiles.append(flat.reshape(n_heads, t, 2 * t - 1)[:, :, :t])
    return jnp.stack(tiles, axis=1)


def _near_bias_tiles_t(bias_tab, causal_mask):
    t = ATT_TILE
    tiles = _bias_tiles_t(bias_tab)
    near = tiles[:, :2] - tiles[:, 2:3, :1, :1]
    if causal_mask:
        k = jnp.arange(t)[:, None]
        q = jnp.arange(t)[None, :]
        dist = jnp.stack([d * t + q - k for d in range(2)])
        near = jnp.where(dist[None] >= 0, near, -jnp.inf)
    return near


def _split_w_in(w):
    d = w.shape[0]
    sizes = (2 * A_HEADS * A_QK_DIM, 2 * A_HEADS * A_QK_DIM, A_WIDTH, A_WIDTH, B_HEADS * B_LAT, B_LAT,
             B_WIDTH, IDX_HEADS * IDX_DIM, IDX_DIM, IDX_HEADS)
    assert sum(sizes) == w.shape[1]
    offs = np.cumsum((0,) + sizes)
    qa, ka, va, za, qb, ckv, zb, iq, ik, iw = (w[:, offs[n]:offs[n + 1]] for n in range(len(sizes)))

    def by_head(m):
        return m.reshape(d, 2, A_HEADS, A_QK_DIM).transpose(0, 2, 1, 3).reshape(d, 2 * A_HEADS * A_QK_DIM)

    main = jnp.concatenate([qb, by_head(qa), by_head(ka), za, zb, iq], axis=1)
    small = jnp.concatenate([ckv, ik, ik], axis=1)
    transposed = jnp.concatenate([va, ckv, iw], axis=1).T
    assert main.shape[1] == MAIN_W and small.shape[1] == SMALL_W
    return main.astype(BF16), small.astype(BF16), transposed.astype(BF16)


def kernel(x, norm_pre_g, w_in, lambda_q1, lambda_k1, lambda_q2, lambda_k2, subln_g, kv_norm_g, idx_k_norm_g,
           w_uv, rel_bias, w_out, norm_post_g):
    batch, seq, d_model = x.shape
    t = ATT_TILE
    assert seq % t == 0 and d_model % LANES == 0
    topk = min(TOPK_MAX, seq // 4)
    row = lambda v: v.reshape(1, -1).astype(F32)

    bias_a_t = _near_bias_tiles_t(rel_bias[:, :A_HEADS], causal_mask=True)
    bias_b_t = _near_bias_tiles_t(rel_bias[:, A_HEADS:], causal_mask=False)

    x2 = x.reshape(batch * seq, d_model)
    for l in range(w_in.shape[0]):
        lam_init = 0.8 - 0.6 * math.exp(-0.3 * l)
        w_main, w_small, w_t = _split_w_in(w_in[l])
        ik_g2 = jnp.concatenate([idx_k_norm_g[l], idx_k_norm_g[l]])
        main, ckv, ik2, v_t, ckv_t, iw_t = _input_projection(
            x2, row(norm_pre_g[l]), w_main, w_small, w_t, row(kv_norm_g[l]), row(ik_g2))
        o_a = _attention_a(main, v_t, bias_a_t, row(lambda_q1[l]), row(lambda_k1[l]), row(lambda_q2[l]),
                           row(lambda_k2[l]), row(subln_g[l]), batch, seq, lam_init)
        w_uv_t = jnp.swapaxes(w_uv[l], 1, 2).astype(BF16)
        o_b = _attention_b(main, ckv, ckv_t, ik2, iw_t, bias_b_t, w_uv_t, batch, seq, topk)
        x2 = _output_projection(o_a, o_b, x2, w_out[l].astype(BF16), row(norm_post_g[l]))
    return x2.reshape(batch, seq, d_model)
```

```python
import functools
import math

import jax
import jax.numpy as jnp
import numpy as np
from jax import lax
from jax.experimental import pallas as pl
from jax.experimental.pallas import tpu as pltpu

F32 = jnp.float32
BF16 = jnp.bfloat16
I32 = jnp.int32
I16 = jnp.int16

A_HEADS = 8
A_QK_DIM = 64
A_V_DIM = 2 * A_QK_DIM
A_WIDTH = A_HEADS * A_V_DIM
B_HEADS = 8
B_LAT = 256
B_V_DIM = 128
B_WIDTH = B_HEADS * B_V_DIM
IDX_HEADS = 16
IDX_DIM = 64
TOPK_MAX = 256
N_BUCKETS = 32
MAX_DISTANCE = 128
EPS = 1e-6

LANES = 128
ATT_TILE = 256
A_GROUP = 8
PROJ_TM = 1024
PROJ_TN = 1024
OUT_TM = 512
SMALL_W = 384
INT_MIN = np.int32(-2 ** 31)
HALF = 2 ** 15
PACK = 16
M_INIT = -1e30

_QB, _QA, _KA, _ZA, _ZB, _IQ, _MAIN_END = 0, 16, 24, 32, 40, 48, 56
MAIN_W = _MAIN_END * LANES


def _dot(a, b):
    return jnp.dot(a, b, preferred_element_type=F32)


def _dot_nt(a, b):
    return lax.dot_general(a, b, (((1,), (1,)), ((), ())), preferred_element_type=F32)


def _causal_tile_runs(i, run):
    far, near = False, True
    n_far = jnp.maximum(i - 1, 0)

    def far_pair(jj, c):
        run(2 * jj, (far, far))
        return c

    lax.fori_loop(0, lax.shift_right_logical(n_far, 1), far_pair, 0)
    pl.when(i == 0)(lambda: run(0, (near,)))
    pl.when((i >= 1) & (n_far % 2 == 0))(lambda: run(i - 1, (near, near)))
    pl.when(n_far % 2 == 1)(lambda: run(i - 2, (far, near, near)))


def _rows8_sum(x):
    return x.reshape(x.shape[0] // 8, 8, x.shape[1]).sum(axis=0)


def _proj_kernel(x_ref, g_ref, w_ref, ws_ref, wt_ref, kvg_ref, kvg_col_ref, ikg_ref,
                 main_ref, ckv_ref, ik_ref, vt_ref, ckvt_ref, iwt_ref, h_ref, *, n_main):
    j = pl.program_id(1)
    t = ATT_TILE

    @pl.when(j == 0)
    def _():
        x = x_ref[...]
        ms = jnp.mean(x * x, axis=-1, keepdims=True)
        hb = (x * lax.rsqrt(ms + EPS) * g_ref[...]).astype(BF16)
        h_ref[...] = hb
        s = _dot(hb, ws_ref[...])
        ckv = s[:, :B_LAT]
        ckv_ms = jnp.mean(ckv * ckv, axis=-1, keepdims=True)
        ckv_ref[...] = (ckv * lax.rsqrt(ckv_ms + EPS) * kvg_ref[...]).astype(BF16)
        ik2 = s[:, B_LAT:B_LAT + 2 * IDX_DIM]
        ik_ms = jnp.sum(ik2 * ik2, axis=-1, keepdims=True) * (1.0 / (2 * IDX_DIM))
        ik_ref[...] = (ik2 * lax.rsqrt(ik_ms + EPS) * ikg_ref[...]).astype(BF16)

    @pl.when(j < n_main)
    def _():
        main_ref[...] = _dot(h_ref[...], w_ref[...]).astype(BF16)

    @pl.when(j == n_main)
    def _():
        h = h_ref[...]
        vt = _dot_nt(wt_ref[0:A_WIDTH, :], h).astype(BF16)
        rest = _dot_nt(wt_ref[A_WIDTH:, :], h)
        ckvt = rest[:B_LAT]
        ckvt_ms = jnp.mean(ckvt * ckvt, axis=0, keepdims=True)
        ckvt = (ckvt * lax.rsqrt(ckvt_ms + EPS) * kvg_col_ref[...]).astype(BF16)
        for c in range(vt_ref.shape[0]):
            vt_ref[c] = vt[:, c * t:(c + 1) * t]
            ckvt_ref[c] = ckvt[:, c * t:(c + 1) * t]
        iwt_ref[...] = rest[B_LAT:B_LAT + IDX_HEADS]


def _input_projection(x2, g, w_main, w_small, w_t, kv_g, ik_g2):
    m, d = x2.shape
    t = ATT_TILE
    tm = min(PROJ_TM, m)
    n_main = MAIN_W // PROJ_TN
    once = dict(pipeline_mode=pl.Buffered(1))
    return pl.pallas_call(
        functools.partial(_proj_kernel, n_main=n_main),
        grid=(m // tm, n_main + 1),
        in_specs=[
            pl.BlockSpec((tm, d), lambda i, j: (i, 0)),
            pl.BlockSpec((1, d), lambda i, j: (0, 0)),
            pl.BlockSpec((d, PROJ_TN), lambda i, j: (0, jnp.minimum(j, n_main - 1))),
            pl.BlockSpec((d, SMALL_W), lambda i, j: (0, 0), **once),
            pl.BlockSpec((A_WIDTH + B_LAT + IDX_HEADS, d), lambda i, j: (0, 0), **once),
            pl.BlockSpec((1, B_LAT), lambda i, j: (0, 0)),
            pl.BlockSpec((B_LAT, 1), lambda i, j: (0, 0)),
            pl.BlockSpec((1, 2 * IDX_DIM), lambda i, j: (0, 0)),
        ],
        out_specs=[
            pl.BlockSpec((tm, PROJ_TN), lambda i, j: (i, jnp.minimum(j, n_main - 1))),
            pl.BlockSpec((tm, B_LAT), lambda i, j: (i, 0)),
            pl.BlockSpec((tm, 2 * IDX_DIM), lambda i, j: (i, 0)),
            pl.BlockSpec((tm // t, A_WIDTH, t), lambda i, j: (i, 0, 0)),
            pl.BlockSpec((tm // t, B_LAT, t), lambda i, j: (i, 0, 0)),
            pl.BlockSpec((IDX_HEADS, tm), lambda i, j: (0, i)),
        ],
        out_shape=[
            jax.ShapeDtypeStruct((m, MAIN_W), BF16),
            jax.ShapeDtypeStruct((m, B_LAT), BF16),
            jax.ShapeDtypeStruct((m, 2 * IDX_DIM), BF16),
            jax.ShapeDtypeStruct((m // t, A_WIDTH, t), BF16),
            jax.ShapeDtypeStruct((m // t, B_LAT, t), BF16),
            jax.ShapeDtypeStruct((IDX_HEADS, m), F32),
        ],
        scratch_shapes=[pltpu.VMEM((tm, d), BF16)],
        compiler_params=pltpu.CompilerParams(
            dimension_semantics=("parallel", "arbitrary"), vmem_limit_bytes=56 * 2 ** 20),
        name="input_projection",
    )(x2, g, w_main, w_small, w_t, kv_g, kv_g.reshape(-1, 1), ik_g2)


def _attn_a_kernel(q_ref, k_ref, vt_ref, z_ref, bias_ref, lq1_ref, lk1_ref, lq2_ref, lk2_ref, g_ref,
                   o_ref, qs_ref, m_ref, l_ref, acc_ref, *, lam_init, n_blocks):
    t = ATT_TILE
    lam = (jnp.exp(jnp.sum(lq1_ref[...] * lk1_ref[...], axis=-1, keepdims=True))
           - jnp.exp(jnp.sum(lq2_ref[...] * lk2_ref[...], axis=-1, keepdims=True)) + lam_init)
    lane = lax.broadcasted_iota(I32, (t, LANES), 1)
    zero = jnp.zeros((t, LANES), BF16)
    heads = range(A_GROUP)

    def q_block(i, carry):
        rows = pl.ds(pl.multiple_of(i * t, t), t)
        for g in heads:
            q = q_ref[rows, g * LANES:(g + 1) * LANES] * (A_QK_DIM ** -0.5)
            qs_ref[g, 0:t, :] = jnp.where(lane < A_QK_DIM, q, zero)
            qs_ref[g, t:2 * t, :] = jnp.where(lane >= A_QK_DIM, q, zero)
        m_ref[...] = jnp.full(m_ref.shape, M_INIT, F32)
        l_ref[...] = jnp.zeros(l_ref.shape, F32)
        acc_ref[...] = jnp.zeros(acc_ref.shape, F32)

        def scores(g, j):
            keys = pl.ds(pl.multiple_of(j * t, t), t)
            return _dot_nt(k_ref[keys, g * LANES:(g + 1) * LANES], qs_ref[g])

        def softmax_step(g, j, s, near):
            vtj = vt_ref[j, g * A_V_DIM:(g + 1) * A_V_DIM, :]
            for c_ in range(2):
                cols = slice(c_ * t, (c_ + 1) * t)
                sc = s[:, cols]
                if near:
                    sc = sc + bias_ref[g, i - j]
                m_old = m_ref[g, :, cols]
                m_new = jnp.maximum(m_old, jnp.max(sc, axis=0, keepdims=True))
                alpha = jnp.exp(m_old - m_new)
                p = jnp.exp(sc - m_new)
                l_ref[g, :, cols] = alpha * l_ref[g, :, cols] + jnp.sum(p, axis=0, keepdims=True)
                acc_ref[g, :, cols] = alpha * acc_ref[g, :, cols] + _dot(vtj, p.astype(BF16))
                m_ref[g, :, cols] = m_new

        def kv_tiles(j0, kinds):
            for c_, near in enumerate(kinds):
                s = [scores(g, j0 + c_) for g in heads]
                for g in heads:
                    softmax_step(g, j0 + c_, s[g], near)

        _causal_tile_runs(i, kv_tiles)

        for g in heads:
            inv_l = 1.0 / l_ref[g]
            a_t = (acc_ref[g, :, 0:t] * inv_l[:, 0:t]
                   - lam * (acc_ref[g, :, t:2 * t] * inv_l[:, t:2 * t]))
            y_t = a_t * lax.rsqrt(jnp.mean(a_t * a_t, axis=0, keepdims=True) + EPS)
            z = z_ref[rows, g * LANES:(g + 1) * LANES].astype(F32)
            o_ref[rows, g * LANES:(g + 1) * LANES] = (
                y_t.T * g_ref[...] * (1.0 - lam_init) * (z / (1.0 + jnp.exp(-z)))).astype(BF16)
        return carry

    lax.fori_loop(0, n_blocks, q_block, 0)


def _attention_a(main, v_t, bias_tiles_t, lq1, lk1, lq2, lk2, sub_g, batch, seq, lam_init):
    t = ATT_TILE
    nq = seq // t
    gw = A_GROUP * LANES
    vec = lambda n: pl.BlockSpec((1, n), lambda b, h: (0, 0))
    col = lambda off: pl.BlockSpec((seq, gw), lambda b, h: (b, off // A_GROUP + h))
    return pl.pallas_call(
        functools.partial(_attn_a_kernel, lam_init=lam_init, n_blocks=nq),
        grid=(batch, A_HEADS // A_GROUP),
        in_specs=[
            col(_QA), col(_KA),
            pl.BlockSpec((nq, A_GROUP * A_V_DIM, t), lambda b, h: (b, h, 0)),
            col(_ZA),
            pl.BlockSpec((A_GROUP, 2, t, t), lambda b, h: (h, 0, 0, 0)),
            vec(A_QK_DIM), vec(A_QK_DIM), vec(A_QK_DIM), vec(A_QK_DIM), vec(A_V_DIM),
        ],
        out_specs=pl.BlockSpec((seq, gw), lambda b, h: (b, h)),
        out_shape=jax.ShapeDtypeStruct((batch * seq, A_WIDTH), BF16),
        scratch_shapes=[
            pltpu.VMEM((A_GROUP, 2 * t, LANES), BF16),
            pltpu.VMEM((A_GROUP, 1, 2 * t), F32),
            pltpu.VMEM((A_GROUP, 1, 2 * t), F32),
            pltpu.VMEM((A_GROUP, A_V_DIM, 2 * t), F32),
        ],
        compiler_params=pltpu.CompilerParams(
            dimension_semantics=("parallel", "arbitrary"), vmem_limit_bytes=56 * 2 ** 20),
        name="attention_a",
    )(main, main, v_t, main, bias_tiles_t, lq1, lk1, lq2, lk2, sub_g)


def _attn_b_kernel(qb_ref, iq_ref, iqn_ref, zb_ref, ckv_ref, ckvt_ref, ik_ref, iwt_ref, iwtn_ref, bias_ref,
                   wuvt_ref, o_ref, keys_ref, hi_ref, lo_ref, cand_ref, iqm_ref, qs_ref, m_ref, l_ref, acc_ref,
                   j_ref, thr_ref, cnt_ref, *, topk, seq):
    t = ATT_TILE
    nq = seq // t
    i = pl.program_id(1)
    n_chunks = i + 1
    slot, nslot = i % 2, (i + 1) % 2
    cur = keys_ref.at[slot]
    cur_hi, cur_lo = hi_ref.at[slot], lo_ref.at[slot]

    lane = lax.broadcasted_iota(I32, (t, LANES), 1)
    zero = jnp.zeros((t, LANES), BF16)
    key_minus_query = (lax.broadcasted_iota(I32, (t, t), 0) - lax.broadcasted_iota(I32, (t, t), 1))
    key_row = lax.broadcasted_iota(I32, (t, t), 0)
    idx_scale = (IDX_HEADS ** -0.5) * (IDX_DIM ** -0.5)

    def stage_indexer_queries(src_ref):
        for hp in range(IDX_HEADS // 2):
            pair = src_ref[:, hp * LANES:(hp + 1) * LANES]
            iqm_ref[2 * hp] = jnp.where(lane < IDX_DIM, pair, zero)
            iqm_ref[2 * hp + 1] = jnp.where(lane >= IDX_DIM, pair, zero)

    def index_keys(j, wt, dst_slot, diagonal):
        rows = pl.ds(pl.multiple_of(j * t, t), t)
        ikj = ik_ref[rows, :]
        score = jnp.zeros((t, t), F32)
        for h in range(IDX_HEADS):
            d = _dot_nt(ikj, iqm_ref[h])
            score = score + wt[h:h + 1, :] * jnp.maximum(d, 0.0)
        bits = lax.bitcast_convert_type(score, I32)
        key = bits ^ ((bits >> 31) & np.int32(0x7FFFFFFF))
        if diagonal:
            key = jnp.where(key_minus_query <= 0, key, INT_MIN)
        keys_ref[dst_slot, rows, :] = key
        hi_ref[dst_slot, rows, :] = (key >> 16).astype(I16)
        lo_ref[dst_slot, rows, :] = ((key & np.int32(0xFFFF)) - np.int32(HALF)).astype(I16)

    @pl.when(i == 0)
    def _():
        stage_indexer_queries(iq_ref)
        index_keys(0, iwt_ref[...] * idx_scale, slot, True)

    has_next = i < nq - 1

    @pl.when(has_next)
    def _():
        stage_indexer_queries(iqn_ref)

    wt_next = iwtn_ref[...] * idx_scale
    qs_ref[...] = qb_ref[...] * (B_LAT ** -0.5)

    def count(pred_fn):
        def body(j, c8):
            kj = cur[pl.ds(pl.multiple_of(j * t, t), t), :]
            return c8 + _rows8_sum(pred_fn(kj, j).astype(I32))
        c8 = lax.fori_loop(0, n_chunks, body, jnp.zeros((8, t), I32))
        return jnp.sum(c8, axis=0, keepdims=True)

    def count16(ref, n_static, trial):
        one, nil = jnp.ones((), BF16), jnp.zeros((), BF16)
        trial16 = trial.astype(I16)
        acc = None
        for r in range(n_static * t // PACK):
            hit = jnp.where(ref[r * PACK:(r + 1) * PACK, :] >= trial16, one, nil)
            acc = hit if acc is None else acc + hit
        return jnp.sum(acc.astype(F32), axis=0, keepdims=True).astype(I32)

    def bisect16(ref, n_static, need, cnt_init):
        def bit_step(b, carry):
            cu, cnt_at = carry
            trial_u = cu | lax.shift_left(np.int32(1), np.int32(15) - b)
            cnt = count16(ref, n_static, trial_u - HALF)
            ok = cnt >= need
            return jnp.where(ok, trial_u, cu), jnp.where(ok, cnt, cnt_at)

        cu, cnt = lax.fori_loop(0, 16, bit_step, (jnp.zeros((1, t), I32), cnt_init))
        return cu - HALF, cnt

    def bisect(n_static):
        assert n_static * t // PACK <= 256
        thr_hi, cnt_ge_hi = bisect16(cur_hi, n_static, topk, jnp.zeros((1, t), I32))
        cnt_gt_hi = count16(cur_hi, n_static, jnp.minimum(thr_hi + 1, HALF - 1))
        cnt_gt_hi = jnp.where(thr_hi == HALF - 1, 0, cnt_gt_hi)
        thr_hi16 = thr_hi.astype(I16)
        for j in range(n_static):
            rows = slice(j * t, (j + 1) * t)
            cand_ref[rows, :] = jnp.where(cur_hi[rows, :] == thr_hi16, cur_lo[rows, :], np.int16(-HALF))
        thr_lo, cnt_lo = bisect16(cand_ref, n_static, topk - cnt_gt_hi, cnt_ge_hi - cnt_gt_hi)
        thr_ref[...] = thr_hi * (2 * HALF) + (thr_lo + HALF)
        cnt_ref[...] = cnt_gt_hi + cnt_lo

    for n_static in range(1, nq + 1):
        pl.when(i == n_static - 1)(functools.partial(bisect, n_static))

    thr = thr_ref[...]
    cnt_ge = cnt_ref[...]
    thr_eq = jnp.maximum(thr, INT_MIN + np.int32(1))

    idx_bits = max(1, (seq - 1).bit_length())
    j_ref[...] = jnp.full((1, t), 2 ** idx_bits - 1, I32)

    @pl.when(jnp.max((cnt_ge - topk).astype(F32)) > 0.0)
    def _():
        need = topk - count(lambda kj, j: kj > thr)

        def pos_step(b, jlim):
            trial = jlim + lax.shift_left(np.int32(1), np.int32(idx_bits - 1) - b)
            cnt = count(lambda kj, j: (kj == thr_eq) & (key_row + j * t < trial))
            return jnp.where(cnt < need, trial, jlim)

        j_ref[...] = lax.fori_loop(0, idx_bits, pos_step, jnp.zeros((1, t), I32))

    j_lim = j_ref[...]

    m_ref[...] = jnp.full(m_ref.shape, M_INIT, F32)
    l_ref[...] = jnp.zeros(l_ref.shape, F32)
    acc_ref[...] = jnp.zeros(acc_ref.shape, F32)

    def att_tiles(with_indexer):
        def run(j0, kinds):
            for c_, near in enumerate(kinds):
                j = j0 + c_
                rows = pl.ds(pl.multiple_of(j * t, t), t)
                kj = cur[rows, :]
                sel = (kj > thr) | ((kj == thr_eq) & (key_row + j * t <= j_lim))
                sel_bias = jnp.where(sel, 0.0, -jnp.inf)
                ckvj = ckv_ref[rows, :]
                ckvtj = ckvt_ref[j]
                raw = [_dot_nt(ckvj, qs_ref[:, h * B_LAT:(h + 1) * B_LAT]) for h in range(B_HEADS)]
                if with_indexer:
                    index_keys(j, wt_next, nslot, False)
                ps, alphas = [], []
                for h in range(B_HEADS):
                    lg = raw[h] + sel_bias
                    if near:
                        lg = lg + bias_ref[h, i - j]
                    m_old = m_ref[h]
                    m_new = jnp.maximum(m_old, jnp.max(lg, axis=0, keepdims=True))
                    alpha = jnp.exp(m_old - m_new)
                    p = jnp.exp(lg - m_new)
                    l_ref[h] = alpha * l_ref[h] + jnp.sum(p, axis=0, keepdims=True)
                    m_ref[h] = m_new
                    ps.append(p.astype(BF16))
                    alphas.append(alpha)
                pvs = [_dot(ckvtj, ps[h]) for h in range(B_HEADS)]
                for h in range(B_HEADS):
                    acc_ref[h] = alphas[h] * acc_ref[h] + pvs[h]
        return run

    @pl.when(has_next)
    def _():
        _causal_tile_runs(i, att_tiles(True))
        index_keys(i + 1, wt_next, nslot, True)

    @pl.when(jnp.logical_not(has_next))
    def _():
        _causal_tile_runs(i, att_tiles(False))

    for h in range(B_HEADS):
        o_lat_t = (acc_ref[h] * (1.0 / l_ref[h])).astype(BF16)
        o_t = _dot(wuvt_ref[h], o_lat_t)
        z = zb_ref[:, h * B_V_DIM:(h + 1) * B_V_DIM].astype(F32)
        o_ref[:, h * B_V_DIM:(h + 1) * B_V_DIM] = (o_t.T * (z / (1.0 + jnp.exp(-z)))).astype(BF16)


def _attention_b(main, ckv, ckv_t, ik2, iw_t, bias_tiles_t, w_uv_t, batch, seq, topk):
    t = ATT_TILE
    nq = seq // t
    tile = lambda b, i: b * nq + i
    next_tile = lambda b, i: b * nq + jnp.minimum(i + 1, nq - 1)
    iq_col = _IQ * LANES // (IDX_HEADS * IDX_DIM)
    return pl.pallas_call(
        functools.partial(_attn_b_kernel, topk=topk, seq=seq),
        grid=(batch, nq),
        in_specs=[
            pl.BlockSpec((t, B_HEADS * B_LAT), lambda b, i: (tile(b, i), _QB * LANES // (B_HEADS * B_LAT))),
            pl.BlockSpec((t, IDX_HEADS * IDX_DIM), lambda b, i: (tile(b, i), iq_col)),
            pl.BlockSpec((t, IDX_HEADS * IDX_DIM), lambda b, i: (next_tile(b, i), iq_col)),
            pl.BlockSpec((t, B_WIDTH), lambda b, i: (tile(b, i), _ZB * LANES // B_WIDTH)),
            pl.BlockSpec((seq, B_LAT), lambda b, i: (b, 0)),
            pl.BlockSpec((nq, B_LAT, t), lambda b, i: (b, 0, 0)),
            pl.BlockSpec((seq, 2 * IDX_DIM), lambda b, i: (b, 0)),
            pl.BlockSpec((IDX_HEADS, t), lambda b, i: (0, tile(b, i))),
            pl.BlockSpec((IDX_HEADS, t), lambda b, i: (0, next_tile(b, i))),
            pl.BlockSpec((B_HEADS, 2, t, t), lambda b, i: (0, 0, 0, 0)),
            pl.BlockSpec((B_HEADS, B_V_DIM, B_LAT), lambda b, i: (0, 0, 0)),
        ],
        out_specs=pl.BlockSpec((t, B_WIDTH), lambda b, i: (tile(b, i), 0)),
        out_shape=jax.ShapeDtypeStruct((batch * seq, B_WIDTH), BF16),
        scratch_shapes=[
            pltpu.VMEM((2, seq, t), I32),
            pltpu.VMEM((2, seq, t), I16),
            pltpu.VMEM((2, seq, t), I16),
            pltpu.VMEM((seq, t), I16),
            pltpu.VMEM((IDX_HEADS, t, LANES), BF16),
            pltpu.VMEM((t, B_HEADS * B_LAT), BF16),
            pltpu.VMEM((B_HEADS, 1, t), F32),
            pltpu.VMEM((B_HEADS, 1, t), F32),
            pltpu.VMEM((B_HEADS, B_LAT, t), F32),
            pltpu.VMEM((1, t), I32),
            pltpu.VMEM((1, t), I32),
            pltpu.VMEM((1, t), I32),
        ],
        compiler_params=pltpu.CompilerParams(
            dimension_semantics=("parallel", "arbitrary"), vmem_limit_bytes=52 * 2 ** 20),
        name="attention_b",
    )(main, main, main, main, ckv, ckv_t, ik2, iw_t, iw_t, bias_tiles_t, w_uv_t)


def _out_kernel(oa_ref, ob_ref, x_ref, w_ref, g_ref, o_ref):
    y = _dot(oa_ref[...], w_ref[0:A_WIDTH, :]) + _dot(ob_ref[...], w_ref[A_WIDTH:A_WIDTH + B_WIDTH, :])
    ms = jnp.mean(y * y, axis=-1, keepdims=True)
    o_ref[...] = x_ref[...] + y * lax.rsqrt(ms + EPS) * g_ref[...]


def _output_projection(oa, ob, x2, w_out, g):
    m, d = x2.shape
    tm = min(OUT_TM, m)
    return pl.pallas_call(
        _out_kernel,
        grid=(m // tm,),
        in_specs=[
            pl.BlockSpec((tm, A_WIDTH), lambda i: (i, 0)),
            pl.BlockSpec((tm, B_WIDTH), lambda i: (i, 0)),
            pl.BlockSpec((tm, d), lambda i: (i, 0)),
            pl.BlockSpec((A_WIDTH + B_WIDTH, d), lambda i: (0, 0)),
            pl.BlockSpec((1, d), lambda i: (0, 0)),
        ],
        out_specs=pl.BlockSpec((tm, d), lambda i: (i, 0)),
        out_shape=jax.ShapeDtypeStruct((m, d), F32),
        compiler_params=pltpu.CompilerParams(
            dimension_semantics=("parallel",), vmem_limit_bytes=52 * 2 ** 20),
        name="output_projection",
    )(oa, ob, x2, w_out, g)


def _t5_bucket(dist):
    n = jnp.maximum(dist, 0)
    max_exact = N_BUCKETS // 2
    nf = jnp.maximum(n, 1).astype(F32)
    large = max_exact + (jnp.log(nf / max_exact) / math.log(MAX_DISTANCE / max_exact)
                         * (N_BUCKETS - max_exact)).astype(I32)
    large = jnp.minimum(large, N_BUCKETS - 1)
    return jnp.where(n < max_exact, n, large)


def _bias_tiles_t(bias_tab):
    t = ATT_TILE
    assert t > MAX_DISTANCE
    n_heads = bias_tab.shape[1]
    by_dist = bias_tab[_t5_bucket(jnp.arange(-t, 3 * t))].T
    tiles = []
    for d in range(3):
        u = by_dist[:, d * t:d * t + 2 * t]
        flat = jnp.tile(u, (1, t + 1))[:, t:t + t * (2 * t - 1)]
        tiles.append(flat.reshape(n_heads, t, 2 * t - 1)[:, :, :t])
    return jnp.stack(tiles, axis=1)


def _near_bias_tiles_t(bias_tab, causal_mask):
    t = ATT_TILE
    tiles = _bias_tiles_t(bias_tab)
    near = tiles[:, :2] - tiles[:, 2:3, :1, :1]
    if causal_mask:
        k = jnp.arange(t)[:, None]
        q = jnp.arange(t)[None, :]
        dist = jnp.stack([d * t + q - k for d in range(2)])
        near = jnp.where(dist[None] >= 0, near, -jnp.inf)
    return near


def _split_w_in(w):
    d = w.shape[0]
    sizes = (2 * A_HEADS * A_QK_DIM, 2 * A_HEADS * A_QK_DIM, A_WIDTH, A_WIDTH, B_HEADS * B_LAT, B_LAT,
             B_WIDTH, IDX_HEADS * IDX_DIM, IDX_DIM, IDX_HEADS)
    assert sum(sizes) == w.shape[1]
    offs = np.cumsum((0,) + sizes)
    qa, ka, va, za, qb, ckv, zb, iq, ik, iw = (w[:, offs[n]:offs[n + 1]] for n in range(len(sizes)))

    def by_head(m):
        return m.reshape(d, 2, A_HEADS, A_QK_DIM).transpose(0, 2, 1, 3).reshape(d, 2 * A_HEADS * A_QK_DIM)

    main = jnp.concatenate([qb, by_head(qa), by_head(ka), za, zb, iq], axis=1)
    small = jnp.concatenate([ckv, ik, ik], axis=1)
    transposed = jnp.concatenate([va, ckv, iw], axis=1).T
    assert main.shape[1] == MAIN_W and small.shape[1] == SMALL_W
    return main.astype(BF16), small.astype(BF16), transposed.astype(BF16)


def kernel(x, norm_pre_g, w_in, lambda_q1, lambda_k1, lambda_q2, lambda_k2, subln_g, kv_norm_g, idx_k_norm_g,
           w_uv, rel_bias, w_out, norm_post_g):
    batch, seq, d_model = x.shape
    t = ATT_TILE
    assert seq % t == 0 and d_model % LANES == 0
    topk = min(TOPK_MAX, seq // 4)
    row = lambda v: v.reshape(1, -1).astype(F32)

    bias_a_t = _near_bias_tiles_t(rel_bias[:, :A_HEADS], causal_mask=True)
    bias_b_t = _near_bias_tiles_t(rel_bias[:, A_HEADS:], causal_mask=False)

    x2 = x.reshape(batch * seq, d_model)
    for l in range(w_in.shape[0]):
        lam_init = 0.8 - 0.6 * math.exp(-0.3 * l)
        w_main, w_small, w_t = _split_w_in(w_in[l])
        ik_g2 = jnp.concatenate([idx_k_norm_g[l], idx_k_norm_g[l]])
        main, ckv, ik2, v_t, ckv_t, iw_t = _input_projection(
            x2, row(norm_pre_g[l]), w_main, w_small, w_t, row(kv_norm_g[l]), row(ik_g2))
        o_a = _attention_a(main, v_t, bias_a_t, row(lambda_q1[l]), row(lambda_k1[l]), row(lambda_q2[l]),
                           row(lambda_k2[l]), row(subln_g[l]), batch, seq, lam_init)
        w_uv_t = jnp.swapaxes(w_uv[l], 1, 2).astype(BF16)
        o_b = _attention_b(main, ckv, ckv_t, ik2, iw_t, bias_b_t, w_uv_t, batch, seq, topk)
        x2 = _output_projection(o_a, o_b, x2, w_out[l].astype(BF16), row(norm_post_g[l]))
    return x2.reshape(batch, seq, d_model)
```

```python
import functools
import math

import jax
import jax.numpy as jnp
import numpy as np
from jax import lax
from jax.experimental import pallas as pl
from jax.experimental.pallas import tpu as pltpu

F32 = jnp.float32
BF16 = jnp.bfloat16
I32 = jnp.int32
I16 = jnp.int16

A_HEADS = 8
A_QK_DIM = 64
A_V_DIM = 2 * A_QK_DIM
A_WIDTH = A_HEADS * A_V_DIM
B_HEADS = 8
B_LAT = 256
B_V_DIM = 128
B_WIDTH = B_HEADS * B_V_DIM
IDX_HEADS = 16
IDX_DIM = 64
TOPK_MAX = 256
N_BUCKETS = 32
MAX_DISTANCE = 128
EPS = 1e-6

LANES = 128
ATT_TILE = 256
A_GROUP = 8
PROJ_TM = 1024
PROJ_TN = 1024
OUT_TM = 512
SMALL_W = 384
INT_MIN = np.int32(-2 ** 31)
HALF = 2 ** 15
PACK = 16
M_INIT = -1e30

_QB, _QA, _KA, _ZA, _ZB, _IQ, _MAIN_END = 0, 16, 24, 32, 40, 48, 56
MAIN_W = _MAIN_END * LANES


def _dot(a, b):
    return jnp.dot(a, b, preferred_element_type=F32)


def _dot_nt(a, b):
    return lax.dot_general(a, b, (((1,), (1,)), ((), ())), preferred_element_type=F32)


def _causal_tile_runs(i, run):
    far, near = False, True
    n_far = jnp.maximum(i - 1, 0)

    def far_pair(jj, c):
        run(2 * jj, (far, far))
        return c

    lax.fori_loop(0, lax.shift_right_logical(n_far, 1), far_pair, 0)
    pl.when(i == 0)(lambda: run(0, (near,)))
    pl.when((i >= 1) & (n_far % 2 == 0))(lambda: run(i - 1, (near, near)))
    pl.when(n_far % 2 == 1)(lambda: run(i - 2, (far, near, near)))


def _rows8_sum(x):
    return x.reshape(x.shape[0] // 8, 8, x.shape[1]).sum(axis=0)


def _proj_kernel(x_ref, g_ref, w_ref, ws_ref, wt_ref, kvg_ref, kvg_col_ref, ikg_ref,
                 main_ref, ckv_ref, ik_ref, vt_ref, ckvt_ref, iwt_ref, h_ref, *, n_main):
    j = pl.program_id(1)
    t = ATT_TILE

    @pl.when(j == 0)
    def _():
        x = x_ref[...]
        ms = jnp.mean(x * x, axis=-1, keepdims=True)
        hb = (x * lax.rsqrt(ms + EPS) * g_ref[...]).astype(BF16)
        h_ref[...] = hb
        s = _dot(hb, ws_ref[...])
        ckv = s[:, :B_LAT]
        ckv_ms = jnp.mean(ckv * ckv, axis=-1, keepdims=True)
        ckv_ref[...] = (ckv * lax.rsqrt(ckv_ms + EPS) * kvg_ref[...]).astype(BF16)
        ik2 = s[:, B_LAT:B_LAT + 2 * IDX_DIM]
        ik_ms = jnp.sum(ik2 * ik2, axis=-1, keepdims=True) * (1.0 / (2 * IDX_DIM))
        ik_ref[...] = (ik2 * lax.rsqrt(ik_ms + EPS) * ikg_ref[...]).astype(BF16)

    @pl.when(j < n_main)
    def _():
        main_ref[...] = _dot(h_ref[...], w_ref[...]).astype(BF16)

    @pl.when(j == n_main)
    def _():
        h = h_ref[...]
        vt = _dot_nt(wt_ref[0:A_WIDTH, :], h).astype(BF16)
        rest = _dot_nt(wt_ref[A_WIDTH:, :], h)
        ckvt = rest[:B_LAT]
        ckvt_ms = jnp.mean(ckvt * ckvt, axis=0, keepdims=True)
        ckvt = (ckvt * lax.rsqrt(ckvt_ms + EPS) * kvg_col_ref[...]).astype(BF16)
        for c in range(vt_ref.shape[0]):
            vt_ref[c] = vt[:, c * t:(c + 1) * t]
            ckvt_ref[c] = ckvt[:, c * t:(c + 1) * t]
        iwt_ref[...] = rest[B_LAT:B_LAT + IDX_HEADS]


def _input_projection(x2, g, w_main, w_small, w_t, kv_g, ik_g2):
    m, d = x2.shape
    t = ATT_TILE
    tm = min(PROJ_TM, m)
    n_main = MAIN_W // PROJ_TN
    once = dict(pipeline_mode=pl.Buffered(1))
    return pl.pallas_call(
        functools.partial(_proj_kernel, n_main=n_main),
        grid=(m // tm, n_main + 1),
        in_specs=[
            pl.BlockSpec((tm, d), lambda i, j: (i, 0)),
            pl.BlockSpec((1, d), lambda i, j: (0, 0)),
            pl.BlockSpec((d, PROJ_TN), lambda i, j: (0, jnp.minimum(j, n_main - 1))),
            pl.BlockSpec((d, SMALL_W), lambda i, j: (0, 0), **once),
            pl.BlockSpec((A_WIDTH + B_LAT + IDX_HEADS, d), lambda i, j: (0, 0), **once),
            pl.BlockSpec((1, B_LAT), lambda i, j: (0, 0)),
            pl.BlockSpec((B_LAT, 1), lambda i, j: (0, 0)),
            pl.BlockSpec((1, 2 * IDX_DIM), lambda i, j: (0, 0)),
        ],
        out_specs=[
            pl.BlockSpec((tm, PROJ_TN), lambda i, j: (i, jnp.minimum(j, n_main - 1))),
            pl.BlockSpec((tm, B_LAT), lambda i, j: (i, 0)),
            pl.BlockSpec((tm, 2 * IDX_DIM), lambda i, j: (i, 0)),
            pl.BlockSpec((tm // t, A_WIDTH, t), lambda i, j: (i, 0, 0)),
            pl.BlockSpec((tm // t, B_LAT, t), lambda i, j: (i, 0, 0)),
            pl.BlockSpec((IDX_HEADS, tm), lambda i, j: (0, i)),
        ],
        out_shape=[
            jax.ShapeDtypeStruct((m, MAIN_W), BF16),
            jax.ShapeDtypeStruct((m, B_LAT), BF16),
            jax.ShapeDtypeStruct((m, 2 * IDX_DIM), BF16),
            jax.ShapeDtypeStruct((m // t, A_WIDTH, t), BF16),
            jax.ShapeDtypeStruct((m // t, B_LAT, t), BF16),
            jax.ShapeDtypeStruct((IDX_HEADS, m), F32),
        ],
        scratch_shapes=[pltpu.VMEM((tm, d), BF16)],
        compiler_params=pltpu.CompilerParams(
            dimension_semantics=("parallel", "arbitrary"), vmem_limit_bytes=56 * 2 ** 20),
        name="input_projection",
    )(x2, g, w_main, w_small, w_t, kv_g, kv_g.reshape(-1, 1), ik_g2)


def _attn_a_kernel(q_ref, k_ref, vt_ref, z_ref, bias_ref, lq1_ref, lk1_ref, lq2_ref, lk2_ref, g_ref,
                   o_ref, qs_ref, m_ref, l_ref, acc_ref, *, lam_init, n_blocks):
    t = ATT_TILE
    lam = (jnp.exp(jnp.sum(lq1_ref[...] * lk1_ref[...], axis=-1, keepdims=True))
           - jnp.exp(jnp.sum(lq2_ref[...] * lk2_ref[...], axis=-1, keepdims=True)) + lam_init)
    lane = lax.broadcasted_iota(I32, (t, LANES), 1)
    zero = jnp.zeros((t, LANES), BF16)
    heads = range(A_GROUP)

    def q_block(i, carry):
        rows = pl.ds(pl.multiple_of(i * t, t), t)
        for g in heads:
            q = q_ref[rows, g * LANES:(g + 1) * LANES]
            qs_ref[g, 0:t, :] = jnp.where(lane < A_QK_DIM, q, zero)
            qs_ref[g, t:2 * t, :] = jnp.where(lane >= A_QK_DIM, q, zero)
        m_ref[...] = jnp.full(m_ref.shape, M_INIT, F32)
        l_ref[...] = jnp.zeros(l_ref.shape, F32)
        acc_ref[...] = jnp.zeros(acc_ref.shape, F32)

        def scores(g, j):
            keys = pl.ds(pl.multiple_of(j * t, t), t)
            return _dot_nt(k_ref[keys, g * LANES:(g + 1) * LANES], qs_ref[g])

        def softmax_step(g, j, s, near):
            vtj = vt_ref[j, g * A_V_DIM:(g + 1) * A_V_DIM, :]
            for c_ in range(2):
                cols = slice(c_ * t, (c_ + 1) * t)
                sc = s[:, cols]
                if near:
                    sc = sc + bias_ref[g, i - j]
                m_old = m_ref[g, :, cols]
                m_new = jnp.maximum(m_old, jnp.max(sc, axis=0, keepdims=True))
                alpha = jnp.exp(m_old - m_new)
                p = jnp.exp(sc - m_new)
                l_ref[g, :, cols] = alpha * l_ref[g, :, cols] + jnp.sum(p, axis=0, keepdims=True)
                acc_ref[g, :, cols] = alpha * acc_ref[g, :, cols] + _dot(vtj, p.astype(BF16))
                m_ref[g, :, cols] = m_new

        def kv_tiles(j0, kinds):
            for c_, near in enumerate(kinds):
                s = [scores(g, j0 + c_) for g in heads]
                for g in heads:
                    softmax_step(g, j0 + c_, s[g], near)

        _causal_tile_runs(i, kv_tiles)

        for g in heads:
            inv_l = 1.0 / l_ref[g]
            a_t = (acc_ref[g, :, 0:t] * inv_l[:, 0:t]
                   - lam * (acc_ref[g, :, t:2 * t] * inv_l[:, t:2 * t]))
            y_t = a_t * lax.rsqrt(jnp.mean(a_t * a_t, axis=0, keepdims=True) + EPS)
            z = z_ref[rows, g * LANES:(g + 1) * LANES].astype(F32)
            o_ref[rows, g * LANES:(g + 1) * LANES] = (
                y_t.T * g_ref[...] * (1.0 - lam_init) * (z / (1.0 + jnp.exp(-z)))).astype(BF16)
        return carry

    lax.fori_loop(0, n_blocks, q_block, 0)


def _attention_a(main, v_t, bias_tiles_t, lq1, lk1, lq2, lk2, sub_g, batch, seq, lam_init):
    t = ATT_TILE
    nq = seq // t
    gw = A_GROUP * LANES
    vec = lambda n: pl.BlockSpec((1, n), lambda b, h: (0, 0))
    col = lambda off: pl.BlockSpec((seq, gw), lambda b, h: (b, off // A_GROUP + h))
    return pl.pallas_call(
        functools.partial(_attn_a_kernel, lam_init=lam_init, n_blocks=nq),
        grid=(batch, A_HEADS // A_GROUP),
        in_specs=[
            col(_QA), col(_KA),
            pl.BlockSpec((nq, A_GROUP * A_V_DIM, t), lambda b, h: (b, h, 0)),
            col(_ZA),
            pl.BlockSpec((A_GROUP, 2, t, t), lambda b, h: (h, 0, 0, 0)),
            vec(A_QK_DIM), vec(A_QK_DIM), vec(A_QK_DIM), vec(A_QK_DIM), vec(A_V_DIM),
        ],
        out_specs=pl.BlockSpec((seq, gw), lambda b, h: (b, h)),
        out_shape=jax.ShapeDtypeStruct((batch * seq, A_WIDTH), BF16),
        scratch_shapes=[
            pltpu.VMEM((A_GROUP, 2 * t, LANES), BF16),
            pltpu.VMEM((A_GROUP, 1, 2 * t), F32),
            pltpu.VMEM((A_GROUP, 1, 2 * t), F32),
            pltpu.VMEM((A_GROUP, A_V_DIM, 2 * t), F32),
        ],
        compiler_params=pltpu.CompilerParams(
            dimension_semantics=("parallel", "arbitrary"), vmem_limit_bytes=56 * 2 ** 20),
        name="attention_a",
    )(main, main, v_t, main, bias_tiles_t, lq1, lk1, lq2, lk2, sub_g)


def _attn_b_kernel(qb_ref, iq_ref, iqn_ref, zb_ref, ckv_ref, ckvt_ref, ik_ref, iwt_ref, iwtn_ref, bias_ref,
                   wuvt_ref, o_ref, keys_ref, hi_ref, lo_ref, cand_ref, iqm_ref, m_ref, l_ref, acc_ref,
                   j_ref, thr_ref, cnt_ref, *, topk, seq):
    t = ATT_TILE
    nq = seq // t
    i = pl.program_id(1)
    n_chunks = i + 1
    slot, nslot = i % 2, (i + 1) % 2
    cur = keys_ref.at[slot]
    cur_hi, cur_lo = hi_ref.at[slot], lo_ref.at[slot]

    lane = lax.broadcasted_iota(I32, (t, LANES), 1)
    zero = jnp.zeros((t, LANES), BF16)
    key_minus_query = (lax.broadcasted_iota(I32, (t, t), 0) - lax.broadcasted_iota(I32, (t, t), 1))
    key_row = lax.broadcasted_iota(I32, (t, t), 0)
    idx_scale = (IDX_HEADS ** -0.5) * (IDX_DIM ** -0.5)

    def stage_indexer_queries(src_ref):
        for hp in range(IDX_HEADS // 2):
            pair = src_ref[:, hp * LANES:(hp + 1) * LANES]
            iqm_ref[2 * hp] = jnp.where(lane < IDX_DIM, pair, zero)
            iqm_ref[2 * hp + 1] = jnp.where(lane >= IDX_DIM, pair, zero)

    def index_keys(j, wt, dst_slot, diagonal):
        rows = pl.ds(pl.multiple_of(j * t, t), t)
        ikj = ik_ref[rows, :]
        score = jnp.zeros((t, t), F32)
        for h in range(IDX_HEADS):
            d = _dot_nt(ikj, iqm_ref[h])
            score = score + wt[h:h + 1, :] * jnp.maximum(d, 0.0)
        bits = lax.bitcast_convert_type(score, I32)
        key = bits ^ ((bits >> 31) & np.int32(0x7FFFFFFF))
        if diagonal:
            key = jnp.where(key_minus_query <= 0, key, INT_MIN)
        keys_ref[dst_slot, rows, :] = key
        hi_ref[dst_slot, rows, :] = (key >> 16).astype(I16)
        lo_ref[dst_slot, rows, :] = ((key & np.int32(0xFFFF)) - np.int32(HALF)).astype(I16)

    @pl.when(i == 0)
    def _():
        stage_indexer_queries(iq_ref)
        index_keys(0, iwt_ref[...] * idx_scale, slot, True)

    has_next = i < nq - 1

    @pl.when(has_next)
    def _():
        stage_indexer_queries(iqn_ref)

    wt_next = iwtn_ref[...] * idx_scale

    def count(pred_fn):
        def body(j, c8):
            kj = cur[pl.ds(pl.multiple_of(j * t, t), t), :]
            return c8 + _rows8_sum(pred_fn(kj, j).astype(I32))
        c8 = lax.fori_loop(0, n_chunks, body, jnp.zeros((8, t), I32))
        return jnp.sum(c8, axis=0, keepdims=True)

    def count16(ref, n_static, trial):
        one, nil = jnp.ones((), BF16), jnp.zeros((), BF16)
        trial16 = trial.astype(I16)
        acc = None
        for r in range(n_static * t // PACK):
            hit = jnp.where(ref[r * PACK:(r + 1) * PACK, :] >= trial16, one, nil)
            acc = hit if acc is None else acc + hit
        return jnp.sum(acc.astype(F32), axis=0, keepdims=True).astype(I32)

    def bisect16(ref, n_static, need, cnt_init):
        def bit_step(b, carry):
            cu, cnt_at = carry
            trial_u = cu | lax.shift_left(np.int32(1), np.int32(15) - b)
            cnt = count16(ref, n_static, trial_u - HALF)
            ok = cnt >= need
            return jnp.where(ok, trial_u, cu), jnp.where(ok, cnt, cnt_at)

        cu, cnt = lax.fori_loop(0, 16, bit_step, (jnp.zeros((1, t), I32), cnt_init))
        return cu - HALF, cnt

    def bisect(n_static):
        assert n_static * t // PACK <= 256
        thr_hi, cnt_ge_hi = bisect16(cur_hi, n_static, topk, jnp.zeros((1, t), I32))
        cnt_gt_hi = count16(cur_hi, n_static, jnp.minimum(thr_hi + 1, HALF - 1))
        cnt_gt_hi = jnp.where(thr_hi == HALF - 1, 0, cnt_gt_hi)
        thr_hi16 = thr_hi.astype(I16)
        for j in range(n_static):
            rows = slice(j * t, (j + 1) * t)
            cand_ref[rows, :] = jnp.where(cur_hi[rows, :] == thr_hi16, cur_lo[rows, :], np.int16(-HALF))
        thr_lo, cnt_lo = bisect16(cand_ref, n_static, topk - cnt_gt_hi, cnt_ge_hi - cnt_gt_hi)
        thr_ref[...] = thr_hi * (2 * HALF) + (thr_lo + HALF)
        cnt_ref[...] = cnt_gt_hi + cnt_lo

    for n_static in range(1, nq + 1):
        pl.when(i == n_static - 1)(functools.partial(bisect, n_static))

    thr = thr_ref[...]
    cnt_ge = cnt_ref[...]
    thr_eq = jnp.maximum(thr, INT_MIN + np.int32(1))

    idx_bits = max(1, (seq - 1).bit_length())
    j_ref[...] = jnp.full((1, t), 2 ** idx_bits - 1, I32)

    @pl.when(jnp.max((cnt_ge - topk).astype(F32)) > 0.0)
    def _():
        need = topk - count(lambda kj, j: kj > thr)

        def pos_step(b, jlim):
            trial = jlim + lax.shift_left(np.int32(1), np.int32(idx_bits - 1) - b)
            cnt = count(lambda kj, j: (kj == thr_eq) & (key_row + j * t < trial))
            return jnp.where(cnt < need, trial, jlim)

        j_ref[...] = lax.fori_loop(0, idx_bits, pos_step, jnp.zeros((1, t), I32))

    j_lim = j_ref[...]

    m_ref[...] = jnp.full(m_ref.shape, M_INIT, F32)
    l_ref[...] = jnp.zeros(l_ref.shape, F32)
    acc_ref[...] = jnp.zeros(acc_ref.shape, F32)

    def att_tiles(with_indexer):
        def run(j0, kinds):
            for c_, near in enumerate(kinds):
                j = j0 + c_
                rows = pl.ds(pl.multiple_of(j * t, t), t)
                kj = cur[rows, :]
                sel = (kj > thr) | ((kj == thr_eq) & (key_row + j * t <= j_lim))
                sel_bias = jnp.where(sel, 0.0, -jnp.inf)
                ckvj = ckv_ref[rows, :]
                ckvtj = ckvt_ref[j]
                raw = [_dot_nt(ckvj, qb_ref[:, h * B_LAT:(h + 1) * B_LAT]) for h in range(B_HEADS)]
                if with_indexer:
                    index_keys(j, wt_next, nslot, False)
                ps, alphas = [], []
                for h in range(B_HEADS):
                    lg = raw[h] + sel_bias
                    if near:
                        lg = lg + bias_ref[h, i - j]
                    m_old = m_ref[h]
                    m_new = jnp.maximum(m_old, jnp.max(lg, axis=0, keepdims=True))
                    alpha = jnp.exp(m_old - m_new)
                    p = jnp.exp(lg - m_new)
                    l_ref[h] = alpha * l_ref[h] + jnp.sum(p, axis=0, keepdims=True)
                    m_ref[h] = m_new
                    ps.append(p.astype(BF16))
                    alphas.append(alpha)
                pvs = [_dot(ckvtj, ps[h]) for h in range(B_HEADS)]
                for h in range(B_HEADS):
                    acc_ref[h] = alphas[h] * acc_ref[h] + pvs[h]
        return run

    @pl.when(has_next)
    def _():
        _causal_tile_runs(i, att_tiles(True))
        index_keys(i + 1, wt_next, nslot, True)

    @pl.when(jnp.logical_not(has_next))
    def _():
        _causal_tile_runs(i, att_tiles(False))

    for h in range(B_HEADS):
        o_t = _dot(wuvt_ref[h], acc_ref[h].astype(BF16)) * (1.0 / l_ref[h])
        z = zb_ref[:, h * B_V_DIM:(h + 1) * B_V_DIM].astype(F32)
        o_ref[:, h * B_V_DIM:(h + 1) * B_V_DIM] = (o_t.T * (z / (1.0 + jnp.exp(-z)))).astype(BF16)


def _attention_b(main, ckv, ckv_t, ik2, iw_t, bias_tiles_t, w_uv_t, batch, seq, topk):
    t = ATT_TILE
    nq = seq // t
    tile = lambda b, i: b * nq + i
    next_tile = lambda b, i: b * nq + jnp.minimum(i + 1, nq - 1)
    iq_col = _IQ * LANES // (IDX_HEADS * IDX_DIM)
    return pl.pallas_call(
        functools.partial(_attn_b_kernel, topk=topk, seq=seq),
        grid=(batch, nq),
        in_specs=[
            pl.BlockSpec((t, B_HEADS * B_LAT), lambda b, i: (tile(b, i), _QB * LANES // (B_HEADS * B_LAT))),
            pl.BlockSpec((t, IDX_HEADS * IDX_DIM), lambda b, i: (tile(b, i), iq_col)),
            pl.BlockSpec((t, IDX_HEADS * IDX_DIM), lambda b, i: (next_tile(b, i), iq_col)),
            pl.BlockSpec((t, B_WIDTH), lambda b, i: (tile(b, i), _ZB * LANES // B_WIDTH)),
            pl.BlockSpec((seq, B_LAT), lambda b, i: (b, 0)),
            pl.BlockSpec((nq, B_LAT, t), lambda b, i: (b, 0, 0)),
            pl.BlockSpec((seq, 2 * IDX_DIM), lambda b, i: (b, 0)),
            pl.BlockSpec((IDX_HEADS, t), lambda b, i: (0, tile(b, i))),
            pl.BlockSpec((IDX_HEADS, t), lambda b, i: (0, next_tile(b, i))),
            pl.BlockSpec((B_HEADS, 2, t, t), lambda b, i: (0, 0, 0, 0)),
            pl.BlockSpec((B_HEADS, B_V_DIM, B_LAT), lambda b, i: (0, 0, 0)),
        ],
        out_specs=pl.BlockSpec((t, B_WIDTH), lambda b, i: (tile(b, i), 0)),
        out_shape=jax.ShapeDtypeStruct((batch * seq, B_WIDTH), BF16),
        scratch_shapes=[
            pltpu.VMEM((2, seq, t), I32),
            pltpu.VMEM((2, seq, t), I16),
            pltpu.VMEM((2, seq, t), I16),
            pltpu.VMEM((seq, t), I16),
            pltpu.VMEM((IDX_HEADS, t, LANES), BF16),
            pltpu.VMEM((B_HEADS, 1, t), F32),
            pltpu.VMEM((B_HEADS, 1, t), F32),
            pltpu.VMEM((B_HEADS, B_LAT, t), F32),
            pltpu.VMEM((1, t), I32),
            pltpu.VMEM((1, t), I32),
            pltpu.VMEM((1, t), I32),
        ],
        compiler_params=pltpu.CompilerParams(
            dimension_semantics=("parallel", "arbitrary"), vmem_limit_bytes=52 * 2 ** 20),
        name="attention_b",
    )(main, main, main, main, ckv, ckv_t, ik2, iw_t, iw_t, bias_tiles_t, w_uv_t)


def _out_kernel(oa_ref, ob_ref, x_ref, w_ref, g_ref, o_ref):
    y = _dot(oa_ref[...], w_ref[0:A_WIDTH, :]) + _dot(ob_ref[...], w_ref[A_WIDTH:A_WIDTH + B_WIDTH, :])
    ms = jnp.mean(y * y, axis=-1, keepdims=True)
    o_ref[...] = x_ref[...] + y * lax.rsqrt(ms + EPS) * g_ref[...]


def _output_projection(oa, ob, x2, w_out, g):
    m, d = x2.shape
    tm = min(OUT_TM, m)
    return pl.pallas_call(
        _out_kernel,
        grid=(m // tm,),
        in_specs=[
            pl.BlockSpec((tm, A_WIDTH), lambda i: (i, 0)),
            pl.BlockSpec((tm, B_WIDTH), lambda i: (i, 0)),
            pl.BlockSpec((tm, d), lambda i: (i, 0)),
            pl.BlockSpec((A_WIDTH + B_WIDTH, d), lambda i: (0, 0)),
            pl.BlockSpec((1, d), lambda i: (0, 0)),
        ],
        out_specs=pl.BlockSpec((tm, d), lambda i: (i, 0)),
        out_shape=jax.ShapeDtypeStruct((m, d), F32),
        compiler_params=pltpu.CompilerParams(
            dimension_semantics=("parallel",), vmem_limit_bytes=52 * 2 ** 20),
        name="output_projection",
    )(oa, ob, x2, w_out, g)


def _t5_bucket(dist):
    n = jnp.maximum(dist, 0)
    max_exact = N_BUCKETS // 2
    nf = jnp.maximum(n, 1).astype(F32)
    large = max_exact + (jnp.log(nf / max_exact) / math.log(MAX_DISTANCE / max_exact)
                         * (N_BUCKETS - max_exact)).astype(I32)
    large = jnp.minimum(large, N_BUCKETS - 1)
    return jnp.where(n < max_exact, n, large)


def _near_bias_tiles_t(bias_tab):
    t = ATT_TILE
    assert t > MAX_DISTANCE
    n_heads = bias_tab.shape[1]
    by_dist = bias_tab[_t5_bucket(jnp.arange(-t, 2 * t + 1))].T
    by_dist = by_dist - by_dist[:, 3 * t:]
    tiles = []
    for d in range(2):
        u = by_dist[:, d * t:d * t + 2 * t]
        flat = jnp.tile(u, (1, t + 1))[:, t:t + t * (2 * t - 1)]
        tiles.append(flat.reshape(n_heads, t, 2 * t - 1)[:, :, :t])
    return jnp.stack(tiles, axis=1)


def _causal_masked(tiles):
    t = ATT_TILE
    k = jnp.arange(t)[:, None]
    q = jnp.arange(t)[None, :]
    dist = jnp.stack([d * t + q - k for d in range(2)])
    return jnp.where(dist[None] >= 0, tiles, -jnp.inf)


def _split_w_in(w):
    d = w.shape[0]
    sizes = (2 * A_HEADS * A_QK_DIM, 2 * A_HEADS * A_QK_DIM, A_WIDTH, A_WIDTH, B_HEADS * B_LAT, B_LAT,
             B_WIDTH, IDX_HEADS * IDX_DIM, IDX_DIM, IDX_HEADS)
    assert sum(sizes) == w.shape[1]
    offs = np.cumsum((0,) + sizes)
    qa, ka, va, za, qb, ckv, zb, iq, ik, iw = (w[:, offs[n]:offs[n + 1]] for n in range(len(sizes)))

    def by_head(m):
        return m.reshape(d, 2, A_HEADS, A_QK_DIM).transpose(0, 2, 1, 3).reshape(d, 2 * A_HEADS * A_QK_DIM)

    main = jnp.concatenate([qb * (B_LAT ** -0.5), by_head(qa) * (A_QK_DIM ** -0.5), by_head(ka), za, zb, iq],
                           axis=1)
    small = jnp.concatenate([ckv, ik, ik], axis=1)
    transposed = jnp.concatenate([va, ckv, iw], axis=1).T
    assert main.shape[1] == MAIN_W and small.shape[1] == SMALL_W
    return main.astype(BF16), small.astype(BF16), transposed.astype(BF16)


def kernel(x, norm_pre_g, w_in, lambda_q1, lambda_k1, lambda_q2, lambda_k2, subln_g, kv_norm_g, idx_k_norm_g,
           w_uv, rel_bias, w_out, norm_post_g):
    batch, seq, d_model = x.shape
    t = ATT_TILE
    assert seq % t == 0 and d_model % LANES == 0
    topk = min(TOPK_MAX, seq // 4)
    row = lambda v: v.reshape(1, -1).astype(F32)

    bias_t = _near_bias_tiles_t(rel_bias)
    bias_a_t = _causal_masked(bias_t[:A_HEADS])
    bias_b_t = bias_t[A_HEADS:]

    x2 = x.reshape(batch * seq, d_model)
    for l in range(w_in.shape[0]):
        lam_init = 0.8 - 0.6 * math.exp(-0.3 * l)
        w_main, w_small, w_t = _split_w_in(w_in[l])
        ik_g2 = jnp.concatenate([idx_k_norm_g[l], idx_k_norm_g[l]])
        main, ckv, ik2, v_t, ckv_t, iw_t = _input_projection(
            x2, row(norm_pre_g[l]), w_main, w_small, w_t, row(kv_norm_g[l]), row(ik_g2))
        o_a = _attention_a(main, v_t, bias_a_t, row(lambda_q1[l]), row(lambda_k1[l]), row(lambda_q2[l]),
                           row(lambda_k2[l]), row(subln_g[l]), batch, seq, lam_init)
        w_uv_t = jnp.swapaxes(w_uv[l], 1, 2).astype(BF16)
        o_b = _attention_b(main, ckv, ckv_t, ik2, iw_t, bias_b_t, w_uv_t, batch, seq, topk)
        x2 = _output_projection(o_a, o_b, x2, w_out[l].astype(BF16), row(norm_post_g[l]))
    return x2.reshape(batch, seq, d_model)
```

```python
import functools
import math

import jax
import jax.numpy as jnp
import numpy as np
from jax import lax
from jax.experimental import pallas as pl
from jax.experimental.pallas import tpu as pltpu

F32 = jnp.float32
BF16 = jnp.bfloat16
I32 = jnp.int32
I16 = jnp.int16

A_HEADS = 8
A_QK_DIM = 64
A_V_DIM = 2 * A_QK_DIM
A_WIDTH = A_HEADS * A_V_DIM
B_HEADS = 8
B_LAT = 256
B_V_DIM = 128
B_WIDTH = B_HEADS * B_V_DIM
IDX_HEADS = 16
IDX_DIM = 64
TOPK_MAX = 256
N_BUCKETS = 32
MAX_DISTANCE = 128
EPS = 1e-6

LANES = 128
ATT_TILE = 256
A_GROUP = 8
PROJ_TM = 1024
PROJ_TN = 1024
OUT_TM = 512
SMALL_W = 384
INT_MIN = np.int32(-2 ** 31)
HALF = 2 ** 15
PACK = 16
M_INIT = -1e30

_QB, _QA, _KA, _ZA, _ZB, _IQ, _MAIN_END = 0, 16, 24, 32, 40, 48, 56
MAIN_W = _MAIN_END * LANES


def _dot(a, b):
    return jnp.dot(a, b, preferred_element_type=F32)


def _dot_nt(a, b):
    return lax.dot_general(a, b, (((1,), (1,)), ((), ())), preferred_element_type=F32)


def _causal_tile_runs(i, run):
    far, near = False, True
    n_far = jnp.maximum(i - 1, 0)

    def far_pair(jj, c):
        run(2 * jj, (far, far))
        return c

    lax.fori_loop(0, lax.shift_right_logical(n_far, 1), far_pair, 0)
    pl.when(i == 0)(lambda: run(0, (near,)))
    pl.when((i >= 1) & (n_far % 2 == 0))(lambda: run(i - 1, (near, near)))
    pl.when(n_far % 2 == 1)(lambda: run(i - 2, (far, near, near)))


def _rows8_sum(x):
    return x.reshape(x.shape[0] // 8, 8, x.shape[1]).sum(axis=0)


def _proj_kernel(x_ref, g_ref, w_ref, ws_ref, wt_ref, kvg_ref, kvg_col_ref, ikg_ref,
                 main_ref, ckv_ref, ik_ref, vt_ref, ckvt_ref, iwt_ref, h_ref, *, n_main):
    j = pl.program_id(1)
    t = ATT_TILE

    @pl.when(j == 0)
    def _():
        x = x_ref[...]
        ms = jnp.mean(x * x, axis=-1, keepdims=True)
        hb = (x * lax.rsqrt(ms + EPS) * g_ref[...]).astype(BF16)
        h_ref[...] = hb
        s = _dot(hb, ws_ref[...])
        ckv = s[:, :B_LAT]
        ckv_ms = jnp.mean(ckv * ckv, axis=-1, keepdims=True)
        ckv_ref[...] = (ckv * lax.rsqrt(ckv_ms + EPS) * kvg_ref[...]).astype(BF16)
        ik2 = s[:, B_LAT:B_LAT + 2 * IDX_DIM]
        ik_ms = jnp.sum(ik2 * ik2, axis=-1, keepdims=True) * (1.0 / (2 * IDX_DIM))
        ik_ref[...] = (ik2 * lax.rsqrt(ik_ms + EPS) * ikg_ref[...]).astype(BF16)

    @pl.when(j < n_main)
    def _():
        main_ref[...] = _dot(h_ref[...], w_ref[...]).astype(BF16)

    @pl.when(j == n_main)
    def _():
        h = h_ref[...]
        vt = _dot_nt(wt_ref[0:A_WIDTH, :], h).astype(BF16)
        rest = _dot_nt(wt_ref[A_WIDTH:, :], h)
        ckvt = rest[:B_LAT]
        ckvt_ms = jnp.mean(ckvt * ckvt, axis=0, keepdims=True)
        ckvt = (ckvt * lax.rsqrt(ckvt_ms + EPS) * kvg_col_ref[...]).astype(BF16)
        for c in range(vt_ref.shape[0]):
            vt_ref[c] = vt[:, c * t:(c + 1) * t]
            ckvt_ref[c] = ckvt[:, c * t:(c + 1) * t]
        iwt_ref[...] = rest[B_LAT:B_LAT + IDX_HEADS]


def _input_projection(x2, g, w_main, w_small, w_t, kv_g, ik_g2):
    m, d = x2.shape
    t = ATT_TILE
    tm = min(PROJ_TM, m)
    n_main = MAIN_W // PROJ_TN
    once = dict(pipeline_mode=pl.Buffered(1))
    return pl.pallas_call(
        functools.partial(_proj_kernel, n_main=n_main),
        grid=(m // tm, n_main + 1),
        in_specs=[
            pl.BlockSpec((tm, d), lambda i, j: (i, 0)),
            pl.BlockSpec((1, d), lambda i, j: (0, 0)),
            pl.BlockSpec((d, PROJ_TN), lambda i, j: (0, jnp.minimum(j, n_main - 1))),
            pl.BlockSpec((d, SMALL_W), lambda i, j: (0, 0), **once),
            pl.BlockSpec((A_WIDTH + B_LAT + IDX_HEADS, d), lambda i, j: (0, 0), **once),
            pl.BlockSpec((1, B_LAT), lambda i, j: (0, 0)),
            pl.BlockSpec((B_LAT, 1), lambda i, j: (0, 0)),
            pl.BlockSpec((1, 2 * IDX_DIM), lambda i, j: (0, 0)),
        ],
        out_specs=[
            pl.BlockSpec((tm, PROJ_TN), lambda i, j: (i, jnp.minimum(j, n_main - 1))),
            pl.BlockSpec((tm, B_LAT), lambda i, j: (i, 0)),
            pl.BlockSpec((tm, 2 * IDX_DIM), lambda i, j: (i, 0)),
            pl.BlockSpec((tm // t, A_WIDTH, t), lambda i, j: (i, 0, 0)),
            pl.BlockSpec((tm // t, B_LAT, t), lambda i, j: (i, 0, 0)),
            pl.BlockSpec((IDX_HEADS, tm), lambda i, j: (0, i)),
        ],
        out_shape=[
            jax.ShapeDtypeStruct((m, MAIN_W), BF16),
            jax.ShapeDtypeStruct((m, B_LAT), BF16),
            jax.ShapeDtypeStruct((m, 2 * IDX_DIM), BF16),
            jax.ShapeDtypeStruct((m // t, A_WIDTH, t), BF16),
            jax.ShapeDtypeStruct((m // t, B_LAT, t), BF16),
            jax.ShapeDtypeStruct((IDX_HEADS, m), F32),
        ],
        scratch_shapes=[pltpu.VMEM((tm, d), BF16)],
        compiler_params=pltpu.CompilerParams(
            dimension_semantics=("parallel", "arbitrary"), vmem_limit_bytes=56 * 2 ** 20),
        name="input_projection",
    )(x2, g, w_main, w_small, w_t, kv_g, kv_g.reshape(-1, 1), ik_g2)


def _attn_a_kernel(q_ref, k_ref, vt_ref, z_ref, bias_ref, lq1_ref, lk1_ref, lq2_ref, lk2_ref, g_ref,
                   o_ref, qs_ref, m_ref, l_ref, acc_ref, *, lam_init, n_blocks):
    t = ATT_TILE
    lam = (jnp.exp(jnp.sum(lq1_ref[...] * lk1_ref[...], axis=-1, keepdims=True))
           - jnp.exp(jnp.sum(lq2_ref[...] * lk2_ref[...], axis=-1, keepdims=True)) + lam_init)
    lane = lax.broadcasted_iota(I32, (t, LANES), 1)
    zero = jnp.zeros((t, LANES), BF16)
    heads = range(A_GROUP)

    def q_block(i, carry):
        rows = pl.ds(pl.multiple_of(i * t, t), t)
        for g in heads:
            q = q_ref[rows, g * LANES:(g + 1) * LANES]
            qs_ref[g, 0:t, :] = jnp.where(lane < A_QK_DIM, q, zero)
            qs_ref[g, t:2 * t, :] = jnp.where(lane >= A_QK_DIM, q, zero)
        m_ref[...] = jnp.full(m_ref.shape, M_INIT, F32)
        l_ref[...] = jnp.zeros(l_ref.shape, F32)
        acc_ref[...] = jnp.zeros(acc_ref.shape, F32)

        def scores(g, j):
            keys = pl.ds(pl.multiple_of(j * t, t), t)
            return _dot_nt(k_ref[keys, g * LANES:(g + 1) * LANES], qs_ref[g])

        def softmax_step(g, j, s, near):
            vtj = vt_ref[j, g * A_V_DIM:(g + 1) * A_V_DIM, :]
            for c_ in range(2):
                cols = slice(c_ * t, (c_ + 1) * t)
                sc = s[:, cols]
                if near:
                    sc = sc + bias_ref[g, i - j]
                m_old = m_ref[g, :, cols]
                m_new = jnp.maximum(m_old, jnp.max(sc, axis=0, keepdims=True))
                alpha = jnp.exp(m_old - m_new)
                p = jnp.exp(sc - m_new)
                l_ref[g, :, cols] = alpha * l_ref[g, :, cols] + jnp.sum(p, axis=0, keepdims=True)
                acc_ref[g, :, cols] = alpha * acc_ref[g, :, cols] + _dot(vtj, p.astype(BF16))
                m_ref[g, :, cols] = m_new

        def kv_tiles(j0, kinds):
            for c_, near in enumerate(kinds):
                s = [scores(g, j0 + c_) for g in heads]
                for g in heads:
                    softmax_step(g, j0 + c_, s[g], near)

        _causal_tile_runs(i, kv_tiles)

        for g in heads:
            inv_l = 1.0 / l_ref[g]
            a_t = (acc_ref[g, :, 0:t] * inv_l[:, 0:t]
                   - lam * (acc_ref[g, :, t:2 * t] * inv_l[:, t:2 * t]))
            y_t = a_t * lax.rsqrt(jnp.mean(a_t * a_t, axis=0, keepdims=True) + EPS)
            z = z_ref[rows, g * LANES:(g + 1) * LANES].astype(F32)
            o_ref[rows, g * LANES:(g + 1) * LANES] = (
                y_t.T * g_ref[...] * (1.0 - lam_init) * (z / (1.0 + jnp.exp(-z)))).astype(BF16)
        return carry

    lax.fori_loop(0, n_blocks, q_block, 0)


def _attention_a(main, v_t, bias_tiles_t, lq1, lk1, lq2, lk2, sub_g, batch, seq, lam_init):
    t = ATT_TILE
    nq = seq // t
    gw = A_GROUP * LANES
    vec = lambda n: pl.BlockSpec((1, n), lambda b, h: (0, 0))
    col = lambda off: pl.BlockSpec((seq, gw), lambda b, h: (b, off // A_GROUP + h))
    return pl.pallas_call(
        functools.partial(_attn_a_kernel, lam_init=lam_init, n_blocks=nq),
        grid=(batch, A_HEADS // A_GROUP),
        in_specs=[
            col(_QA), col(_KA),
            pl.BlockSpec((nq, A_GROUP * A_V_DIM, t), lambda b, h: (b, h, 0)),
            col(_ZA),
            pl.BlockSpec((A_GROUP, 2, t, t), lambda b, h: (h, 0, 0, 0)),
            vec(A_QK_DIM), vec(A_QK_DIM), vec(A_QK_DIM), vec(A_QK_DIM), vec(A_V_DIM),
        ],
        out_specs=pl.BlockSpec((seq, gw), lambda b, h: (b, h)),
        out_shape=jax.ShapeDtypeStruct((batch * seq, A_WIDTH), BF16),
        scratch_shapes=[
            pltpu.VMEM((A_GROUP, 2 * t, LANES), BF16),
            pltpu.VMEM((A_GROUP, 1, 2 * t), F32),
            pltpu.VMEM((A_GROUP, 1, 2 * t), F32),
            pltpu.VMEM((A_GROUP, A_V_DIM, 2 * t), F32),
        ],
        compiler_params=pltpu.CompilerParams(
            dimension_semantics=("parallel", "arbitrary"), vmem_limit_bytes=56 * 2 ** 20),
        name="attention_a",
    )(main, main, v_t, main, bias_tiles_t, lq1, lk1, lq2, lk2, sub_g)


def _attn_b_kernel(qb_ref, iq_ref, iqn_ref, zb_ref, ckv_ref, ckvt_ref, ik_ref, iwt_ref, iwtn_ref, bias_ref,
                   wuvt_ref, o_ref, keys_ref, hi_ref, lo_ref, cand_ref, iqm_ref, m_ref, l_ref, acc_ref,
                   j_ref, thr_ref, cnt_ref, *, topk, seq):
    t = ATT_TILE
    nq = seq // t
    i = pl.program_id(1)
    n_chunks = i + 1
    slot, nslot = i % 2, (i + 1) % 2
    cur = keys_ref.at[slot]
    cur_hi, cur_lo = hi_ref.at[slot], lo_ref.at[slot]

    lane = lax.broadcasted_iota(I32, (t, LANES), 1)
    zero = jnp.zeros((t, LANES), BF16)
    key_minus_query = (lax.broadcasted_iota(I32, (t, t), 0) - lax.broadcasted_iota(I32, (t, t), 1))
    key_row = lax.broadcasted_iota(I32, (t, t), 0)
    idx_scale = (IDX_HEADS ** -0.5) * (IDX_DIM ** -0.5)

    def stage_indexer_queries(src_ref):
        for hp in range(IDX_HEADS // 2):
            pair = src_ref[:, hp * LANES:(hp + 1) * LANES]
            iqm_ref[2 * hp] = jnp.where(lane < IDX_DIM, pair, zero)
            iqm_ref[2 * hp + 1] = jnp.where(lane >= IDX_DIM, pair, zero)

    def index_keys(j, wt, dst_slot, diagonal):
        rows = pl.ds(pl.multiple_of(j * t, t), t)
        ikj = ik_ref[rows, :]
        score = jnp.zeros((t, t), F32)
        for h in range(IDX_HEADS):
            d = _dot_nt(ikj, iqm_ref[h])
            score = score + wt[h:h + 1, :] * jnp.maximum(d, 0.0)
        bits = lax.bitcast_convert_type(score, I32)
        key = bits ^ ((bits >> 31) & np.int32(0x7FFFFFFF))
        if diagonal:
            key = jnp.where(key_minus_query <= 0, key, INT_MIN)
        keys_ref[dst_slot, rows, :] = key
        hi_ref[dst_slot, rows, :] = (key >> 16).astype(I16)
        lo_ref[dst_slot, rows, :] = ((key & np.int32(0xFFFF)) - np.int32(HALF)).astype(I16)

    @pl.when(i == 0)
    def _():
        stage_indexer_queries(iq_ref)
        index_keys(0, iwt_ref[...] * idx_scale, slot, True)

    has_next = i < nq - 1

    @pl.when(has_next)
    def _():
        stage_indexer_queries(iqn_ref)

    wt_next = iwtn_ref[...] * idx_scale

    def count(pred_fn):
        def body(j, c8):
            kj = cur[pl.ds(pl.multiple_of(j * t, t), t), :]
            return c8 + _rows8_sum(pred_fn(kj, j).astype(I32))
        c8 = lax.fori_loop(0, n_chunks, body, jnp.zeros((8, t), I32))
        return jnp.sum(c8, axis=0, keepdims=True)

    def count16(ref, n_static, trial):
        one, nil = jnp.ones((), BF16), jnp.zeros((), BF16)
        trial16 = trial.astype(I16)
        acc = None
        for r in range(n_static * t // PACK):
            hit = jnp.where(ref[r * PACK:(r + 1) * PACK, :] >= trial16, one, nil)
            acc = hit if acc is None else acc + hit
        return jnp.sum(acc.astype(F32), axis=0, keepdims=True).astype(I32)

    def bisect16(ref, n_static, need, cnt_init):
        def bit_step(b, carry):
            cu, cnt_at = carry
            trial_u = cu | lax.shift_left(np.int32(1), np.int32(15) - b)
            cnt = count16(ref, n_static, trial_u - HALF)
            ok = cnt >= need
            return jnp.where(ok, trial_u, cu), jnp.where(ok, cnt, cnt_at)

        cu, cnt = lax.fori_loop(0, 16, bit_step, (jnp.zeros((1, t), I32), cnt_init))
        return cu - HALF, cnt

    def bisect(n_static):
        assert n_static * t // PACK <= 256
        thr_hi, cnt_ge_hi = bisect16(cur_hi, n_static, topk, jnp.zeros((1, t), I32))
        cnt_gt_hi = count16(cur_hi, n_static, jnp.minimum(thr_hi + 1, HALF - 1))
        cnt_gt_hi = jnp.where(thr_hi == HALF - 1, 0, cnt_gt_hi)
        thr_hi16 = thr_hi.astype(I16)
        for j in range(n_static):
            rows = slice(j * t, (j + 1) * t)
            cand_ref[rows, :] = jnp.where(cur_hi[rows, :] == thr_hi16, cur_lo[rows, :], np.int16(-HALF))
        thr_lo, cnt_lo = bisect16(cand_ref, n_static, topk - cnt_gt_hi, cnt_ge_hi - cnt_gt_hi)
        thr_ref[...] = thr_hi * (2 * HALF) + (thr_lo + HALF)
        cnt_ref[...] = cnt_gt_hi + cnt_lo

    for n_static in range(1, nq + 1):
        pl.when(i == n_static - 1)(functools.partial(bisect, n_static))

    thr = thr_ref[...]
    cnt_ge = cnt_ref[...]
    thr_eq = jnp.maximum(thr, INT_MIN + np.int32(1))

    idx_bits = max(1, (seq - 1).bit_length())
    j_ref[...] = jnp.full((1, t), 2 ** idx_bits - 1, I32)

    @pl.when(jnp.max((cnt_ge - topk).astype(F32)) > 0.0)
    def _():
        need = topk - count(lambda kj, j: kj > thr)

        def pos_step(b, jlim):
            trial = jlim + lax.shift_left(np.int32(1), np.int32(idx_bits - 1) - b)
            cnt = count(lambda kj, j: (kj == thr_eq) & (key_row + j * t < trial))
            return jnp.where(cnt < need, trial, jlim)

        j_ref[...] = lax.fori_loop(0, idx_bits, pos_step, jnp.zeros((1, t), I32))

    j_lim = j_ref[...]

    m_ref[...] = jnp.full(m_ref.shape, M_INIT, F32)
    l_ref[...] = jnp.zeros(l_ref.shape, F32)
    acc_ref[...] = jnp.zeros(acc_ref.shape, F32)

    def att_tiles(with_indexer):
        def run(j0, kinds):
            for c_, near in enumerate(kinds):
                j = j0 + c_
                rows = pl.ds(pl.multiple_of(j * t, t), t)
                kj = cur[rows, :]
                sel = (kj > thr) | ((kj == thr_eq) & (key_row + j * t <= j_lim))
                sel_bias = jnp.where(sel, 0.0, -jnp.inf)
                ckvj = ckv_ref[rows, :]
                ckvtj = ckvt_ref[j]
                raw = [_dot_nt(ckvj, qb_ref[:, h * B_LAT:(h + 1) * B_LAT]) for h in range(B_HEADS)]
                if with_indexer:
                    index_keys(j, wt_next, nslot, False)
                    if c_ == 0 and kinds[-1]:
                        index_keys(j0 + len(kinds), wt_next, nslot, True)
                ps, alphas = [], []
                for h in range(B_HEADS):
                    lg = raw[h] + sel_bias
                    if near:
                        lg = lg + bias_ref[h, i - j]
                    m_old = m_ref[h]
                    m_new = jnp.maximum(m_old, jnp.max(lg, axis=0, keepdims=True))
                    alpha = jnp.exp(m_old - m_new)
                    p = jnp.exp(lg - m_new)
                    l_ref[h] = alpha * l_ref[h] + jnp.sum(p, axis=0, keepdims=True)
                    m_ref[h] = m_new
                    ps.append(p.astype(BF16))
                    alphas.append(alpha)
                pvs = [_dot(ckvtj, ps[h]) for h in range(B_HEADS)]
                for h in range(B_HEADS):
                    acc_ref[h] = alphas[h] * acc_ref[h] + pvs[h]
        return run

    @pl.when(has_next)
    def _():
        _causal_tile_runs(i, att_tiles(True))

    @pl.when(jnp.logical_not(has_next))
    def _():
        _causal_tile_runs(i, att_tiles(False))

    for h in range(B_HEADS):
        o_t = _dot(wuvt_ref[h], acc_ref[h].astype(BF16)) * (1.0 / l_ref[h])
        z = zb_ref[:, h * B_V_DIM:(h + 1) * B_V_DIM].astype(F32)
        o_ref[:, h * B_V_DIM:(h + 1) * B_V_DIM] = (o_t.T * (z / (1.0 + jnp.exp(-z)))).astype(BF16)


def _attention_b(main, ckv, ckv_t, ik2, iw_t, bias_tiles_t, w_uv_t, batch, seq, topk):
    t = ATT_TILE
    nq = seq // t
    tile = lambda b, i: b * nq + i
    next_tile = lambda b, i: b * nq + jnp.minimum(i + 1, nq - 1)
    iq_col = _IQ * LANES // (IDX_HEADS * IDX_DIM)
    return pl.pallas_call(
        functools.partial(_attn_b_kernel, topk=topk, seq=seq),
        grid=(batch, nq),
        in_specs=[
            pl.BlockSpec((t, B_HEADS * B_LAT), lambda b, i: (tile(b, i), _QB * LANES // (B_HEADS * B_LAT))),
            pl.BlockSpec((t, IDX_HEADS * IDX_DIM), lambda b, i: (tile(b, i), iq_col)),
            pl.BlockSpec((t, IDX_HEADS * IDX_DIM), lambda b, i: (next_tile(b, i), iq_col)),
            pl.BlockSpec((t, B_WIDTH), lambda b, i: (tile(b, i), _ZB * LANES // B_WIDTH)),
            pl.BlockSpec((seq, B_LAT), lambda b, i: (b, 0)),
            pl.BlockSpec((nq, B_LAT, t), lambda b, i: (b, 0, 0)),
            pl.BlockSpec((seq, 2 * IDX_DIM), lambda b, i: (b, 0)),
            pl.BlockSpec((IDX_HEADS, t), lambda b, i: (0, tile(b, i))),
            pl.BlockSpec((IDX_HEADS, t), lambda b, i: (0, next_tile(b, i))),
            pl.BlockSpec((B_HEADS, 2, t, t), lambda b, i: (0, 0, 0, 0)),
            pl.BlockSpec((B_HEADS, B_V_DIM, B_LAT), lambda b, i: (0, 0, 0)),
        ],
        out_specs=pl.BlockSpec((t, B_WIDTH), lambda b, i: (tile(b, i), 0)),
        out_shape=jax.ShapeDtypeStruct((batch * seq, B_WIDTH), BF16),
        scratch_shapes=[
            pltpu.VMEM((2, seq, t), I32),
            pltpu.VMEM((2, seq, t), I16),
            pltpu.VMEM((2, seq, t), I16),
            pltpu.VMEM((seq, t), I16),
            pltpu.VMEM((IDX_HEADS, t, LANES), BF16),
            pltpu.VMEM((B_HEADS, 1, t), F32),
            pltpu.VMEM((B_HEADS, 1, t), F32),
            pltpu.VMEM((B_HEADS, B_LAT, t), F32),
            pltpu.VMEM((1, t), I32),
            pltpu.VMEM((1, t), I32),
            pltpu.VMEM((1, t), I32),
        ],
        compiler_params=pltpu.CompilerParams(
            dimension_semantics=("parallel", "arbitrary"), vmem_limit_bytes=52 * 2 ** 20),
        name="attention_b",
    )(main, main, main, main, ckv, ckv_t, ik2, iw_t, iw_t, bias_tiles_t, w_uv_t)


def _out_kernel(oa_ref, ob_ref, x_ref, w_ref, g_ref, o_ref):
    y = _dot(oa_ref[...], w_ref[0:A_WIDTH, :]) + _dot(ob_ref[...], w_ref[A_WIDTH:A_WIDTH + B_WIDTH, :])
    ms = jnp.mean(y * y, axis=-1, keepdims=True)
    o_ref[...] = x_ref[...] + y * lax.rsqrt(ms + EPS) * g_ref[...]


def _output_projection(oa, ob, x2, w_out, g):
    m, d = x2.shape
    tm = min(OUT_TM, m)
    return pl.pallas_call(
        _out_kernel,
        grid=(m // tm,),
        in_specs=[
            pl.BlockSpec((tm, A_WIDTH), lambda i: (i, 0)),
            pl.BlockSpec((tm, B_WIDTH), lambda i: (i, 0)),
            pl.BlockSpec((tm, d), lambda i: (i, 0)),
            pl.BlockSpec((A_WIDTH + B_WIDTH, d), lambda i: (0, 0)),
            pl.BlockSpec((1, d), lambda i: (0, 0)),
        ],
        out_specs=pl.BlockSpec((tm, d), lambda i: (i, 0)),
        out_shape=jax.ShapeDtypeStruct((m, d), F32),
        compiler_params=pltpu.CompilerParams(
            dimension_semantics=("parallel",), vmem_limit_bytes=52 * 2 ** 20),
        name="output_projection",
    )(oa, ob, x2, w_out, g)


def _t5_bucket(dist):
    n = jnp.maximum(dist, 0)
    max_exact = N_BUCKETS // 2
    nf = jnp.maximum(n, 1).astype(F32)
    large = max_exact + (jnp.log(nf / max_exact) / math.log(MAX_DISTANCE / max_exact)
                         * (N_BUCKETS - max_exact)).astype(I32)
    large = jnp.minimum(large, N_BUCKETS - 1)
    return jnp.where(n < max_exact, n, large)


def _near_bias_tiles_t(bias_tab):
    t = ATT_TILE
    assert t > MAX_DISTANCE
    n_heads = bias_tab.shape[1]
    by_dist = bias_tab[_t5_bucket(jnp.arange(-t, 2 * t + 1))].T
    by_dist = by_dist - by_dist[:, 3 * t:]
    tiles = []
    for d in range(2):
        u = by_dist[:, d * t:d * t + 2 * t]
        flat = jnp.tile(u, (1, t + 1))[:, t:t + t * (2 * t - 1)]
        tiles.append(flat.reshape(n_heads, t, 2 * t - 1)[:, :, :t])
    return jnp.stack(tiles, axis=1)


def _causal_masked(tiles):
    t = ATT_TILE
    k = jnp.arange(t)[:, None]
    q = jnp.arange(t)[None, :]
    dist = jnp.stack([d * t + q - k for d in range(2)])
    return jnp.where(dist[None] >= 0, tiles, -jnp.inf)


def _split_w_in(w):
    d = w.shape[0]
    sizes = (2 * A_HEADS * A_QK_DIM, 2 * A_HEADS * A_QK_DIM, A_WIDTH, A_WIDTH, B_HEADS * B_LAT, B_LAT,
             B_WIDTH, IDX_HEADS * IDX_DIM, IDX_DIM, IDX_HEADS)
    assert sum(sizes) == w.shape[1]
    offs = np.cumsum((0,) + sizes)
    qa, ka, va, za, qb, ckv, zb, iq, ik, iw = (w[:, offs[n]:offs[n + 1]] for n in range(len(sizes)))

    def by_head(m):
        return m.reshape(d, 2, A_HEADS, A_QK_DIM).transpose(0, 2, 1, 3).reshape(d, 2 * A_HEADS * A_QK_DIM)

    main = jnp.concatenate([qb * (B_LAT ** -0.5), by_head(qa) * (A_QK_DIM ** -0.5), by_head(ka), za, zb, iq],
                           axis=1)
    small = jnp.concatenate([ckv, ik, ik], axis=1)
    transposed = jnp.concatenate([va, ckv, iw], axis=1).astype(BF16).T
    assert main.shape[1] == MAIN_W and small.shape[1] == SMALL_W
    return main.astype(BF16), small.astype(BF16), transposed


def kernel(x, norm_pre_g, w_in, lambda_q1, lambda_k1, lambda_q2, lambda_k2, subln_g, kv_norm_g, idx_k_norm_g,
           w_uv, rel_bias, w_out, norm_post_g):
    batch, seq, d_model = x.shape
    t = ATT_TILE
    assert seq % t == 0 and d_model % LANES == 0
    topk = min(TOPK_MAX, seq // 4)
    row = lambda v: v.reshape(1, -1).astype(F32)

    bias_t = _near_bias_tiles_t(rel_bias)
    bias_a_t = _causal_masked(bias_t[:A_HEADS])
    bias_b_t = bias_t[A_HEADS:]

    x2 = x.reshape(batch * seq, d_model)
    for l in range(w_in.shape[0]):
        lam_init = 0.8 - 0.6 * math.exp(-0.3 * l)
        w_main, w_small, w_t = _split_w_in(w_in[l])
        ik_g2 = jnp.concatenate([idx_k_norm_g[l], idx_k_norm_g[l]])
        main, ckv, ik2, v_t, ckv_t, iw_t = _input_projection(
            x2, row(norm_pre_g[l]), w_main, w_small, w_t, row(kv_norm_g[l]), row(ik_g2))
        o_a = _attention_a(main, v_t, bias_a_t, row(lambda_q1[l]), row(lambda_k1[l]), row(lambda_q2[l]),
                           row(lambda_k2[l]), row(subln_g[l]), batch, seq, lam_init)
        w_uv_t = jnp.swapaxes(w_uv[l], 1, 2).astype(BF16)
        o_b = _attention_b(main, ckv, ckv_t, ik2, iw_t, bias_b_t, w_uv_t, batch, seq, topk)
        x2 = _output_projection(o_a, o_b, x2, w_out[l].astype(BF16), row(norm_post_g[l]))
    return x2.reshape(batch, seq, d_model)
```

```python
import functools
import math

import jax
import jax.numpy as jnp
import numpy as np
from jax import lax
from jax.experimental import pallas as pl
from jax.experimental.pallas import tpu as pltpu

F32 = jnp.float32
BF16 = jnp.bfloat16
I32 = jnp.int32
I16 = jnp.int16

A_HEADS = 8
A_QK_DIM = 64
A_V_DIM = 2 * A_QK_DIM
A_WIDTH = A_HEADS * A_V_DIM
B_HEADS = 8
B_LAT = 256
B_V_DIM = 128
B_WIDTH = B_HEADS * B_V_DIM
IDX_HEADS = 16
IDX_DIM = 64
TOPK_MAX = 256
N_BUCKETS = 32
MAX_DISTANCE = 128
EPS = 1e-6

LANES = 128
ATT_TILE = 256
A_GROUP = 8
PROJ_TM = 1024
PROJ_TN = 1024
OUT_TM = 512
SMALL_W = 384
INT_MIN = np.int32(-2 ** 31)
HALF = 2 ** 15
PACK = 16
M_INIT = -1e30

_QB, _QA, _KA, _ZA, _ZB, _IQ, _MAIN_END = 0, 16, 24, 32, 40, 48, 56
MAIN_W = _MAIN_END * LANES


def _dot(a, b):
    return jnp.dot(a, b, preferred_element_type=F32)


def _dot_nt(a, b):
    return lax.dot_general(a, b, (((1,), (1,)), ((), ())), preferred_element_type=F32)


def _causal_tile_runs(i, run):
    far, near = False, True
    n_far = jnp.maximum(i - 1, 0)

    def far_pair(jj, c):
        run(2 * jj, (far, far))
        return c

    lax.fori_loop(0, lax.shift_right_logical(n_far, 1), far_pair, 0)
    pl.when(i == 0)(lambda: run(0, (near,)))
    pl.when((i >= 1) & (n_far % 2 == 0))(lambda: run(i - 1, (near, near)))
    pl.when(n_far % 2 == 1)(lambda: run(i - 2, (far, near, near)))


def _rows8_sum(x):
    return x.reshape(x.shape[0] // 8, 8, x.shape[1]).sum(axis=0)


def _proj_kernel(x_ref, g_ref, w_ref, ws_ref, wt_ref, kvg_ref, kvg_col_ref, ikg_ref,
                 main_ref, ckv_ref, ik_ref, vt_ref, ckvt_ref, iwt_ref, h_ref, *, n_main):
    j = pl.program_id(1)
    t = ATT_TILE

    @pl.when(j == 0)
    def _():
        x = x_ref[...]
        ms = jnp.mean(x * x, axis=-1, keepdims=True)
        hb = (x * lax.rsqrt(ms + EPS) * g_ref[...]).astype(BF16)
        h_ref[...] = hb
        s = _dot(hb, ws_ref[...])
        ckv = s[:, :B_LAT]
        ckv_ms = jnp.mean(ckv * ckv, axis=-1, keepdims=True)
        ckv_ref[...] = (ckv * lax.rsqrt(ckv_ms + EPS) * kvg_ref[...]).astype(BF16)
        ik2 = s[:, B_LAT:B_LAT + 2 * IDX_DIM]
        ik_ms = jnp.sum(ik2 * ik2, axis=-1, keepdims=True) * (1.0 / (2 * IDX_DIM))
        ik_ref[...] = (ik2 * lax.rsqrt(ik_ms + EPS) * ikg_ref[...]).astype(BF16)

    @pl.when(j < n_main)
    def _():
        main_ref[...] = _dot(h_ref[...], w_ref[...]).astype(BF16)

    @pl.when(j == n_main)
    def _():
        h = h_ref[...]
        vt = _dot_nt(wt_ref[0:A_WIDTH, :], h).astype(BF16)
        rest = _dot_nt(wt_ref[A_WIDTH:, :], h)
        ckvt = rest[:B_LAT]
        ckvt_ms = jnp.mean(ckvt * ckvt, axis=0, keepdims=True)
        ckvt = (ckvt * lax.rsqrt(ckvt_ms + EPS) * kvg_col_ref[...]).astype(BF16)
        for c in range(vt_ref.shape[0]):
            vt_ref[c] = vt[:, c * t:(c + 1) * t]
            ckvt_ref[c] = ckvt[:, c * t:(c + 1) * t]
        iwt_ref[...] = rest[B_LAT:B_LAT + IDX_HEADS]


def _input_projection(x2, g, w_main, w_small, w_t, kv_g, ik_g2):
    m, d = x2.shape
    t = ATT_TILE
    tm = min(PROJ_TM, m)
    n_main = MAIN_W // PROJ_TN
    once = dict(pipeline_mode=pl.Buffered(1))
    return pl.pallas_call(
        functools.partial(_proj_kernel, n_main=n_main),
        grid=(m // tm, n_main + 1),
        in_specs=[
            pl.BlockSpec((tm, d), lambda i, j: (i, 0)),
            pl.BlockSpec((1, d), lambda i, j: (0, 0)),
            pl.BlockSpec((d, PROJ_TN), lambda i, j: (0, jnp.minimum(j, n_main - 1))),
            pl.BlockSpec((d, SMALL_W), lambda i, j: (0, 0), **once),
            pl.BlockSpec((A_WIDTH + B_LAT + IDX_HEADS, d), lambda i, j: (0, 0), **once),
            pl.BlockSpec((1, B_LAT), lambda i, j: (0, 0)),
            pl.BlockSpec((B_LAT, 1), lambda i, j: (0, 0)),
            pl.BlockSpec((1, 2 * IDX_DIM), lambda i, j: (0, 0)),
        ],
        out_specs=[
            pl.BlockSpec((tm, PROJ_TN), lambda i, j: (i, jnp.minimum(j, n_main - 1))),
            pl.BlockSpec((tm, B_LAT), lambda i, j: (i, 0)),
            pl.BlockSpec((tm, 2 * IDX_DIM), lambda i, j: (i, 0)),
            pl.BlockSpec((tm // t, A_WIDTH, t), lambda i, j: (i, 0, 0)),
            pl.BlockSpec((tm // t, B_LAT, t), lambda i, j: (i, 0, 0)),
            pl.BlockSpec((IDX_HEADS, tm), lambda i, j: (0, i)),
        ],
        out_shape=[
            jax.ShapeDtypeStruct((m, MAIN_W), BF16),
            jax.ShapeDtypeStruct((m, B_LAT), BF16),
            jax.ShapeDtypeStruct((m, 2 * IDX_DIM), BF16),
            jax.ShapeDtypeStruct((m // t, A_WIDTH, t), BF16),
            jax.ShapeDtypeStruct((m // t, B_LAT, t), BF16),
            jax.ShapeDtypeStruct((IDX_HEADS, m), F32),
        ],
        scratch_shapes=[pltpu.VMEM((tm, d), BF16)],
        compiler_params=pltpu.CompilerParams(
            dimension_semantics=("parallel", "arbitrary"), vmem_limit_bytes=56 * 2 ** 20),
        name="input_projection",
    )(x2, g, w_main, w_small, w_t, kv_g, kv_g.reshape(-1, 1), ik_g2)


def _attn_a_kernel(q_ref, k_ref, vt_ref, z_ref, bias_ref, lq1_ref, lk1_ref, lq2_ref, lk2_ref, g_ref,
                   o_ref, qs_ref, m_ref, acc_ref, *, lam_init, n_blocks):
    t = ATT_TILE
    lam = (jnp.exp(jnp.sum(lq1_ref[...] * lk1_ref[...], axis=-1, keepdims=True))
           - jnp.exp(jnp.sum(lq2_ref[...] * lk2_ref[...], axis=-1, keepdims=True)) + lam_init)
    lane = lax.broadcasted_iota(I32, (t, LANES), 1)
    zero = jnp.zeros((t, LANES), BF16)
    heads = range(A_GROUP)
    ones_rows = jnp.ones((PACK, t), BF16)

    def q_block(i, carry):
        rows = pl.ds(pl.multiple_of(i * t, t), t)
        for g in heads:
            q = q_ref[rows, g * LANES:(g + 1) * LANES]
            qs_ref[g, 0:t, :] = jnp.where(lane < A_QK_DIM, q, zero)
            qs_ref[g, t:2 * t, :] = jnp.where(lane >= A_QK_DIM, q, zero)
        m_ref[...] = jnp.full(m_ref.shape, M_INIT, F32)
        acc_ref[...] = jnp.zeros(acc_ref.shape, F32)

        def scores(g, j):
            keys = pl.ds(pl.multiple_of(j * t, t), t)
            return _dot_nt(k_ref[keys, g * LANES:(g + 1) * LANES], qs_ref[g])

        def softmax_step(g, j, s, near):
            vtj = jnp.concatenate([vt_ref[j, g * A_V_DIM:(g + 1) * A_V_DIM, :], ones_rows], axis=0)
            for c_ in range(2):
                cols = slice(c_ * t, (c_ + 1) * t)
                sc = s[:, cols]
                if near:
                    sc = sc + bias_ref[g, i - j]
                m_old = m_ref[g, :, cols]
                m_new = jnp.maximum(m_old, jnp.max(sc, axis=0, keepdims=True))
                alpha = jnp.exp(m_old - m_new)
                p = jnp.exp(sc - m_new)
                acc_ref[g, :, cols] = alpha * acc_ref[g, :, cols] + _dot(vtj, p.astype(BF16))
                m_ref[g, :, cols] = m_new

        def kv_tiles(j0, kinds):
            for c_, near in enumerate(kinds):
                s = [scores(g, j0 + c_) for g in heads]
                for g in heads:
                    softmax_step(g, j0 + c_, s[g], near)

        _causal_tile_runs(i, kv_tiles)

        for g in heads:
            inv_l = 1.0 / acc_ref[g, A_V_DIM:A_V_DIM + 1, :]
            a_t = (acc_ref[g, 0:A_V_DIM, 0:t] * inv_l[:, 0:t]
                   - lam * (acc_ref[g, 0:A_V_DIM, t:2 * t] * inv_l[:, t:2 * t]))
            y_t = a_t * lax.rsqrt(jnp.mean(a_t * a_t, axis=0, keepdims=True) + EPS)
            z = z_ref[rows, g * LANES:(g + 1) * LANES].astype(F32)
            o_ref[rows, g * LANES:(g + 1) * LANES] = (
                y_t.T * g_ref[...] * (1.0 - lam_init) * (z / (1.0 + jnp.exp(-z)))).astype(BF16)
        return carry

    lax.fori_loop(0, n_blocks, q_block, 0)


def _attention_a(main, v_t, bias_tiles_t, lq1, lk1, lq2, lk2, sub_g, batch, seq, lam_init):
    t = ATT_TILE
    nq = seq // t
    gw = A_GROUP * LANES
    vec = lambda n: pl.BlockSpec((1, n), lambda b, h: (0, 0))
    col = lambda off: pl.BlockSpec((seq, gw), lambda b, h: (b, off // A_GROUP + h))
    return pl.pallas_call(
        functools.partial(_attn_a_kernel, lam_init=lam_init, n_blocks=nq),
        grid=(batch, A_HEADS // A_GROUP),
        in_specs=[
            col(_QA), col(_KA),
            pl.BlockSpec((nq, A_GROUP * A_V_DIM, t), lambda b, h: (b, h, 0)),
            col(_ZA),
            pl.BlockSpec((A_GROUP, 2, t, t), lambda b, h: (h, 0, 0, 0)),
            vec(A_QK_DIM), vec(A_QK_DIM), vec(A_QK_DIM), vec(A_QK_DIM), vec(A_V_DIM),
        ],
        out_specs=pl.BlockSpec((seq, gw), lambda b, h: (b, h)),
        out_shape=jax.ShapeDtypeStruct((batch * seq, A_WIDTH), BF16),
        scratch_shapes=[
            pltpu.VMEM((A_GROUP, 2 * t, LANES), BF16),
            pltpu.VMEM((A_GROUP, 1, 2 * t), F32),
            pltpu.VMEM((A_GROUP, A_V_DIM + PACK, 2 * t), F32),
        ],
        compiler_params=pltpu.CompilerParams(
            dimension_semantics=("parallel", "arbitrary"), vmem_limit_bytes=56 * 2 ** 20),
        name="attention_a",
    )(main, main, v_t, main, bias_tiles_t, lq1, lk1, lq2, lk2, sub_g)


def _attn_b_kernel(qb_ref, iq_ref, iqn_ref, zb_ref, ckv_ref, ckvt_ref, ik_ref, iwt_ref, iwtn_ref, bias_ref,
                   wuvt_ref, o_ref, keys_ref, hi_ref, lo_ref, cand_ref, iqm_ref, m_ref, l_ref, acc_ref,
                   j_ref, thr_ref, cnt_ref, *, topk, seq):
    t = ATT_TILE
    nq = seq // t
    i = pl.program_id(1)
    n_chunks = i + 1
    slot, nslot = i % 2, (i + 1) % 2
    cur = keys_ref.at[slot]
    cur_hi, cur_lo = hi_ref.at[slot], lo_ref.at[slot]

    lane = lax.broadcasted_iota(I32, (t, LANES), 1)
    zero = jnp.zeros((t, LANES), BF16)
    key_minus_query = (lax.broadcasted_iota(I32, (t, t), 0) - lax.broadcasted_iota(I32, (t, t), 1))
    key_row = lax.broadcasted_iota(I32, (t, t), 0)
    idx_scale = (IDX_HEADS ** -0.5) * (IDX_DIM ** -0.5)

    def stage_indexer_queries(src_ref):
        for hp in range(IDX_HEADS // 2):
            pair = src_ref[:, hp * LANES:(hp + 1) * LANES]
            iqm_ref[2 * hp] = jnp.where(lane < IDX_DIM, pair, zero)
            iqm_ref[2 * hp + 1] = jnp.where(lane >= IDX_DIM, pair, zero)

    def index_keys(j, wt, dst_slot, diagonal):
        rows = pl.ds(pl.multiple_of(j * t, t), t)
        ikj = ik_ref[rows, :]
        score = jnp.zeros((t, t), F32)
        for h in range(IDX_HEADS):
            d = _dot_nt(ikj, iqm_ref[h])
            score = score + wt[h:h + 1, :] * jnp.maximum(d, 0.0)
        bits = lax.bitcast_convert_type(score, I32)
        key = bits ^ ((bits >> 31) & np.int32(0x7FFFFFFF))
        if diagonal:
            key = jnp.where(key_minus_query <= 0, key, INT_MIN)
        keys_ref[dst_slot, rows, :] = key
        hi_ref[dst_slot, rows, :] = (key >> 16).astype(I16)
        lo_ref[dst_slot, rows, :] = ((key & np.int32(0xFFFF)) - np.int32(HALF)).astype(I16)

    @pl.when(i == 0)
    def _():
        stage_indexer_queries(iq_ref)
        index_keys(0, iwt_ref[...] * idx_scale, slot, True)

    has_next = i < nq - 1

    @pl.when(has_next)
    def _():
        stage_indexer_queries(iqn_ref)

    wt_next = iwtn_ref[...] * idx_scale

    def count(pred_fn):
        def body(j, c8):
            kj = cur[pl.ds(pl.multiple_of(j * t, t), t), :]
            return c8 + _rows8_sum(pred_fn(kj, j).astype(I32))
        c8 = lax.fori_loop(0, n_chunks, body, jnp.zeros((8, t), I32))
        return jnp.sum(c8, axis=0, keepdims=True)

    def count16(ref, n_static, trial):
        one, nil = jnp.ones((), BF16), jnp.zeros((), BF16)
        trial16 = trial.astype(I16)
        acc = None
        for r in range(n_static * t // PACK):
            hit = jnp.where(ref[r * PACK:(r + 1) * PACK, :] >= trial16, one, nil)
            acc = hit if acc is None else acc + hit
        return jnp.sum(acc.astype(F32), axis=0, keepdims=True).astype(I32)

    def bisect16(ref, n_static, need, cnt_init):
        def bit_step(b, carry):
            cu, cnt_at = carry
            trial_u = cu | lax.shift_left(np.int32(1), np.int32(15) - b)
            cnt = count16(ref, n_static, trial_u - HALF)
            ok = cnt >= need
            return jnp.where(ok, trial_u, cu), jnp.where(ok, cnt, cnt_at)

        cu, cnt = lax.fori_loop(0, 16, bit_step, (jnp.zeros((1, t), I32), cnt_init))
        return cu - HALF, cnt

    def bisect(n_static):
        assert n_static * t // PACK <= 256
        thr_hi, cnt_ge_hi = bisect16(cur_hi, n_static, topk, jnp.zeros((1, t), I32))
        cnt_gt_hi = count16(cur_hi, n_static, jnp.minimum(thr_hi + 1, HALF - 1))
        cnt_gt_hi = jnp.where(thr_hi == HALF - 1, 0, cnt_gt_hi)
        thr_hi16 = thr_hi.astype(I16)
        for j in range(n_static):
            rows = slice(j * t, (j + 1) * t)
            cand_ref[rows, :] = jnp.where(cur_hi[rows, :] == thr_hi16, cur_lo[rows, :], np.int16(-HALF))
        thr_lo, cnt_lo = bisect16(cand_ref, n_static, topk - cnt_gt_hi, cnt_ge_hi - cnt_gt_hi)
        thr_ref[...] = thr_hi * (2 * HALF) + (thr_lo + HALF)
        cnt_ref[...] = cnt_gt_hi + cnt_lo

    for n_static in range(1, nq + 1):
        pl.when(i == n_static - 1)(functools.partial(bisect, n_static))

    thr = thr_ref[...]
    cnt_ge = cnt_ref[...]
    thr_eq = jnp.maximum(thr, INT_MIN + np.int32(1))

    idx_bits = max(1, (seq - 1).bit_length())
    j_ref[...] = jnp.full((1, t), 2 ** idx_bits - 1, I32)

    @pl.when(jnp.max((cnt_ge - topk).astype(F32)) > 0.0)
    def _():
        need = topk - count(lambda kj, j: kj > thr)

        def pos_step(b, jlim):
            trial = jlim + lax.shift_left(np.int32(1), np.int32(idx_bits - 1) - b)
            cnt = count(lambda kj, j: (kj == thr_eq) & (key_row + j * t < trial))
            return jnp.where(cnt < need, trial, jlim)

        j_ref[...] = lax.fori_loop(0, idx_bits, pos_step, jnp.zeros((1, t), I32))

    j_lim = j_ref[...]

    m_ref[...] = jnp.full(m_ref.shape, M_INIT, F32)
    l_ref[...] = jnp.zeros(l_ref.shape, F32)
    acc_ref[...] = jnp.zeros(acc_ref.shape, F32)

    def att_tiles(with_indexer):
        def run(j0, kinds):
            for c_, near in enumerate(kinds):
                j = j0 + c_
                rows = pl.ds(pl.multiple_of(j * t, t), t)
                kj = cur[rows, :]
                sel = (kj > thr) | ((kj == thr_eq) & (key_row + j * t <= j_lim))
                sel_bias = jnp.where(sel, 0.0, -jnp.inf)
                ckvj = ckv_ref[rows, :]
                ckvtj = ckvt_ref[j]
                raw = [_dot_nt(ckvj, qb_ref[:, h * B_LAT:(h + 1) * B_LAT]) for h in range(B_HEADS)]
                if with_indexer:
                    index_keys(j, wt_next, nslot, False)
                    if c_ == 0 and kinds[-1]:
                        index_keys(j0 + len(kinds), wt_next, nslot, True)
                ps, alphas = [], []
                for h in range(B_HEADS):
                    lg = raw[h] + sel_bias
                    if near:
                        lg = lg + bias_ref[h, i - j]
                    m_old = m_ref[h]
                    m_new = jnp.maximum(m_old, jnp.max(lg, axis=0, keepdims=True))
                    alpha = jnp.exp(m_old - m_new)
                    p = jnp.exp(lg - m_new)
                    l_ref[h] = alpha * l_ref[h] + jnp.sum(p, axis=0, keepdims=True)
                    m_ref[h] = m_new
                    ps.append(p.astype(BF16))
                    alphas.append(alpha)
                pvs = [_dot(ckvtj, ps[h]) for h in range(B_HEADS)]
                for h in range(B_HEADS):
                    acc_ref[h] = alphas[h] * acc_ref[h] + pvs[h]
        return run

    @pl.when(has_next)
    def _():
        _causal_tile_runs(i, att_tiles(True))

    @pl.when(jnp.logical_not(has_next))
    def _():
        _causal_tile_runs(i, att_tiles(False))

    for h in range(B_HEADS):
        o_t = _dot(wuvt_ref[h], acc_ref[h].astype(BF16)) * (1.0 / l_ref[h])
        z = zb_ref[:, h * B_V_DIM:(h + 1) * B_V_DIM].astype(F32)
        o_ref[:, h * B_V_DIM:(h + 1) * B_V_DIM] = (o_t.T * (z / (1.0 + jnp.exp(-z)))).astype(BF16)


def _attention_b(main, ckv, ckv_t, ik2, iw_t, bias_tiles_t, w_uv_t, batch, seq, topk):
    t = ATT_TILE
    nq = seq // t
    tile = lambda b, i: b * nq + i
    next_tile = lambda b, i: b * nq + jnp.minimum(i + 1, nq - 1)
    iq_col = _IQ * LANES // (IDX_HEADS * IDX_DIM)
    return pl.pallas_call(
        functools.partial(_attn_b_kernel, topk=topk, seq=seq),
        grid=(batch, nq),
        in_specs=[
            pl.BlockSpec((t, B_HEADS * B_LAT), lambda b, i: (tile(b, i), _QB * LANES // (B_HEADS * B_LAT))),
            pl.BlockSpec((t, IDX_HEADS * IDX_DIM), lambda b, i: (tile(b, i), iq_col)),
            pl.BlockSpec((t, IDX_HEADS * IDX_DIM), lambda b, i: (next_tile(b, i), iq_col)),
            pl.BlockSpec((t, B_WIDTH), lambda b, i: (tile(b, i), _ZB * LANES // B_WIDTH)),
            pl.BlockSpec((seq, B_LAT), lambda b, i: (b, 0)),
            pl.BlockSpec((nq, B_LAT, t), lambda b, i: (b, 0, 0)),
            pl.BlockSpec((seq, 2 * IDX_DIM), lambda b, i: (b, 0)),
            pl.BlockSpec((IDX_HEADS, t), lambda b, i: (0, tile(b, i))),
            pl.BlockSpec((IDX_HEADS, t), lambda b, i: (0, next_tile(b, i))),
            pl.BlockSpec((B_HEADS, 2, t, t), lambda b, i: (0, 0, 0, 0)),
            pl.BlockSpec((B_HEADS, B_V_DIM, B_LAT), lambda b, i: (0, 0, 0)),
        ],
        out_specs=pl.BlockSpec((t, B_WIDTH), lambda b, i: (tile(b, i), 0)),
        out_shape=jax.ShapeDtypeStruct((batch * seq, B_WIDTH), BF16),
        scratch_shapes=[
            pltpu.VMEM((2, seq, t), I32),
            pltpu.VMEM((2, seq, t), I16),
            pltpu.VMEM((2, seq, t), I16),
            pltpu.VMEM((seq, t), I16),
            pltpu.VMEM((IDX_HEADS, t, LANES), BF16),
            pltpu.VMEM((B_HEADS, 1, t), F32),
            pltpu.VMEM((B_HEADS, 1, t), F32),
            pltpu.VMEM((B_HEADS, B_LAT, t), F32),
            pltpu.VMEM((1, t), I32),
            pltpu.VMEM((1, t), I32),
            pltpu.VMEM((1, t), I32),
        ],
        compiler_params=pltpu.CompilerParams(
            dimension_semantics=("parallel", "arbitrary"), vmem_limit_bytes=52 * 2 ** 20),
        name="attention_b",
    )(main, main, main, main, ckv, ckv_t, ik2, iw_t, iw_t, bias_tiles_t, w_uv_t)


def _out_kernel(oa_ref, ob_ref, x_ref, w_ref, g_ref, o_ref):
    y = _dot(oa_ref[...], w_ref[0:A_WIDTH, :]) + _dot(ob_ref[...], w_ref[A_WIDTH:A_WIDTH + B_WIDTH, :])
    ms = jnp.mean(y * y, axis=-1, keepdims=True)
    o_ref[...] = x_ref[...] + y * lax.rsqrt(ms + EPS) * g_ref[...]


def _output_projection(oa, ob, x2, w_out, g):
    m, d = x2.shape
    tm = min(OUT_TM, m)
    return pl.pallas_call(
        _out_kernel,
        grid=(m // tm,),
        in_specs=[
            pl.BlockSpec((tm, A_WIDTH), lambda i: (i, 0)),
            pl.BlockSpec((tm, B_WIDTH), lambda i: (i, 0)),
            pl.BlockSpec((tm, d), lambda i: (i, 0)),
            pl.BlockSpec((A_WIDTH + B_WIDTH, d), lambda i: (0, 0)),
            pl.BlockSpec((1, d), lambda i: (0, 0)),
        ],
        out_specs=pl.BlockSpec((tm, d), lambda i: (i, 0)),
        out_shape=jax.ShapeDtypeStruct((m, d), F32),
        compiler_params=pltpu.CompilerParams(
            dimension_semantics=("parallel",), vmem_limit_bytes=52 * 2 ** 20),
        name="output_projection",
    )(oa, ob, x2, w_out, g)


def _t5_bucket(dist):
    n = jnp.maximum(dist, 0)
    max_exact = N_BUCKETS // 2
    nf = jnp.maximum(n, 1).astype(F32)
    large = max_exact + (jnp.log(nf / max_exact) / math.log(MAX_DISTANCE / max_exact)
                         * (N_BUCKETS - max_exact)).astype(I32)
    large = jnp.minimum(large, N_BUCKETS - 1)
    return jnp.where(n < max_exact, n, large)


def _near_bias_tiles_t(bias_tab):
    t = ATT_TILE
    assert t > MAX_DISTANCE
    n_heads = bias_tab.shape[1]
    by_dist = bias_tab[_t5_bucket(jnp.arange(-t, 2 * t + 1))].T
    by_dist = by_dist - by_dist[:, 3 * t:]
    tiles = []
    for d in range(2):
        u = by_dist[:, d * t:d * t + 2 * t]
        flat = jnp.tile(u, (1, t + 1))[:, t:t + t * (2 * t - 1)]
        tiles.append(flat.reshape(n_heads, t, 2 * t - 1)[:, :, :t])
    return jnp.stack(tiles, axis=1)


def _causal_masked(tiles):
    t = ATT_TILE
    k = jnp.arange(t)[:, None]
    q = jnp.arange(t)[None, :]
    dist = jnp.stack([d * t + q - k for d in range(2)])
    return jnp.where(dist[None] >= 0, tiles, -jnp.inf)


def _split_w_in(w):
    d = w.shape[0]
    sizes = (2 * A_HEADS * A_QK_DIM, 2 * A_HEADS * A_QK_DIM, A_WIDTH, A_WIDTH, B_HEADS * B_LAT, B_LAT,
             B_WIDTH, IDX_HEADS * IDX_DIM, IDX_DIM, IDX_HEADS)
    assert sum(sizes) == w.shape[1]
    offs = np.cumsum((0,) + sizes)
    qa, ka, va, za, qb, ckv, zb, iq, ik, iw = (w[:, offs[n]:offs[n + 1]] for n in range(len(sizes)))

    def by_head(m):
        return m.reshape(d, 2, A_HEADS, A_QK_DIM).transpose(0, 2, 1, 3).reshape(d, 2 * A_HEADS * A_QK_DIM)

    main = jnp.concatenate([qb * (B_LAT ** -0.5), by_head(qa) * (A_QK_DIM ** -0.5), by_head(ka), za, zb, iq],
                           axis=1)
    small = jnp.concatenate([ckv, ik, ik], axis=1)
    transposed = jnp.concatenate([va, ckv, iw], axis=1).astype(BF16).T
    assert main.shape[1] == MAIN_W and small.shape[1] == SMALL_W
    return main.astype(BF16), small.astype(BF16), transposed


def kernel(x, norm_pre_g, w_in, lambda_q1, lambda_k1, lambda_q2, lambda_k2, subln_g, kv_norm_g, idx_k_norm_g,
           w_uv, rel_bias, w_out, norm_post_g):
    batch, seq, d_model = x.shape
    t = ATT_TILE
    assert seq % t == 0 and d_model % LANES == 0
    topk = min(TOPK_MAX, seq // 4)
    row = lambda v: v.reshape(1, -1).astype(F32)

    bias_t = _near_bias_tiles_t(rel_bias)
    bias_a_t = _causal_masked(bias_t[:A_HEADS])
    bias_b_t = bias_t[A_HEADS:]

    x2 = x.reshape(batch * seq, d_model)
    for l in range(w_in.shape[0]):
        lam_init = 0.8 - 0.6 * math.exp(-0.3 * l)
        w_main, w_small, w_t = _split_w_in(w_in[l])
        ik_g2 = jnp.concatenate([idx_k_norm_g[l], idx_k_norm_g[l]])
        main, ckv, ik2, v_t, ckv_t, iw_t = _input_projection(
            x2, row(norm_pre_g[l]), w_main, w_small, w_t, row(kv_norm_g[l]), row(ik_g2))
        o_a = _attention_a(main, v_t, bias_a_t, row(lambda_q1[l]), row(lambda_k1[l]), row(lambda_q2[l]),
                           row(lambda_k2[l]), row(subln_g[l]), batch, seq, lam_init)
        w_uv_t = jnp.swapaxes(w_uv[l], 1, 2).astype(BF16)
        o_b = _attention_b(main, ckv, ckv_t, ik2, iw_t, bias_b_t, w_uv_t, batch, seq, topk)
        x2 = _output_projection(o_a, o_b, x2, w_out[l].astype(BF16), row(norm_post_g[l]))
    return x2.reshape(batch, seq, d_model)
```

```python
import functools
import math

import jax
import jax.numpy as jnp
import numpy as np
from jax import lax
from jax.experimental import pallas as pl
from jax.experimental.pallas import tpu as pltpu

F32 = jnp.float32
BF16 = jnp.bfloat16
I32 = jnp.int32
I16 = jnp.int16

A_HEADS = 8
A_QK_DIM = 64
A_V_DIM = 2 * A_QK_DIM
A_WIDTH = A_HEADS * A_V_DIM
B_HEADS = 8
B_LAT = 256
B_V_DIM = 128
B_WIDTH = B_HEADS * B_V_DIM
IDX_HEADS = 16
IDX_DIM = 64
TOPK_MAX = 256
N_BUCKETS = 32
MAX_DISTANCE = 128
EPS = 1e-6

LANES = 128
ATT_TILE = 256
A_GROUP = 8
PROJ_TM = 1024
PROJ_TN = 1024
OUT_TM = 512
SMALL_W = 384
INT_MIN = np.int32(-2 ** 31)
HALF = 2 ** 15
PACK = 16
M_INIT = -1e30

_QB, _QA, _KA, _ZA, _ZB, _IQ, _MAIN_END = 0, 16, 24, 32, 40, 48, 56
MAIN_W = _MAIN_END * LANES


def _dot(a, b):
    return jnp.dot(a, b, preferred_element_type=F32)


def _dot_nt(a, b):
    return lax.dot_general(a, b, (((1,), (1,)), ((), ())), preferred_element_type=F32)


def _causal_tile_runs(i, run):
    far, near = False, True
    n_far = jnp.maximum(i - 1, 0)

    def far_pair(jj, c):
        run(2 * jj, (far, far))
        return c

    lax.fori_loop(0, lax.shift_right_logical(n_far, 1), far_pair, 0)
    pl.when(i == 0)(lambda: run(0, (near,)))
    pl.when((i >= 1) & (n_far % 2 == 0))(lambda: run(i - 1, (near, near)))
    pl.when(n_far % 2 == 1)(lambda: run(i - 2, (far, near, near)))


def _rows8_sum(x):
    return x.reshape(x.shape[0] // 8, 8, x.shape[1]).sum(axis=0)


def _proj_kernel(x_ref, g_ref, w_ref, ws_ref, wt_ref, kvg_ref, kvg_col_ref, ikg_ref,
                 main_ref, ckv_ref, ik_ref, vt_ref, ckvt_ref, iwt_ref, h_ref, *, n_main):
    j = pl.program_id(1)
    t = ATT_TILE

    @pl.when(j == 0)
    def _():
        x = x_ref[...]
        ms = jnp.mean(x * x, axis=-1, keepdims=True)
        hb = (x * lax.rsqrt(ms + EPS) * g_ref[...]).astype(BF16)
        h_ref[...] = hb
        s = _dot(hb, ws_ref[...])
        ckv = s[:, :B_LAT]
        ckv_ms = jnp.mean(ckv * ckv, axis=-1, keepdims=True)
        ckv_ref[...] = (ckv * lax.rsqrt(ckv_ms + EPS) * kvg_ref[...]).astype(BF16)
        ik2 = s[:, B_LAT:B_LAT + 2 * IDX_DIM]
        ik_ms = jnp.sum(ik2 * ik2, axis=-1, keepdims=True) * (1.0 / (2 * IDX_DIM))
        ik_ref[...] = (ik2 * lax.rsqrt(ik_ms + EPS) * ikg_ref[...]).astype(BF16)

    @pl.when(j < n_main)
    def _():
        main_ref[...] = _dot(h_ref[...], w_ref[...]).astype(BF16)

    @pl.when(j == n_main)
    def _():
        h = h_ref[...]
        vt = _dot_nt(wt_ref[0:A_WIDTH, :], h).astype(BF16)
        rest = _dot_nt(wt_ref[A_WIDTH:, :], h)
        ckvt = rest[:B_LAT]
        ckvt_ms = jnp.mean(ckvt * ckvt, axis=0, keepdims=True)
        ckvt = (ckvt * lax.rsqrt(ckvt_ms + EPS) * kvg_col_ref[...]).astype(BF16)
        for c in range(vt_ref.shape[0]):
            vt_ref[c] = vt[:, c * t:(c + 1) * t]
            ckvt_ref[c] = ckvt[:, c * t:(c + 1) * t]
        iwt_ref[...] = rest[B_LAT:B_LAT + IDX_HEADS]


def _input_projection(x2, g, w_main, w_small, w_t, kv_g, ik_g2):
    m, d = x2.shape
    t = ATT_TILE
    tm = min(PROJ_TM, m)
    n_main = MAIN_W // PROJ_TN
    once = dict(pipeline_mode=pl.Buffered(1))
    return pl.pallas_call(
        functools.partial(_proj_kernel, n_main=n_main),
        grid=(m // tm, n_main + 1),
        in_specs=[
            pl.BlockSpec((tm, d), lambda i, j: (i, 0)),
            pl.BlockSpec((1, d), lambda i, j: (0, 0)),
            pl.BlockSpec((d, PROJ_TN), lambda i, j: (0, jnp.minimum(j, n_main - 1))),
            pl.BlockSpec((d, SMALL_W), lambda i, j: (0, 0), **once),
            pl.BlockSpec((A_WIDTH + B_LAT + IDX_HEADS, d), lambda i, j: (0, 0), **once),
            pl.BlockSpec((1, B_LAT), lambda i, j: (0, 0)),
            pl.BlockSpec((B_LAT, 1), lambda i, j: (0, 0)),
            pl.BlockSpec((1, 2 * IDX_DIM), lambda i, j: (0, 0)),
        ],
        out_specs=[
            pl.BlockSpec((tm, PROJ_TN), lambda i, j: (i, jnp.minimum(j, n_main - 1))),
            pl.BlockSpec((tm, B_LAT), lambda i, j: (i, 0)),
            pl.BlockSpec((tm, 2 * IDX_DIM), lambda i, j: (i, 0)),
            pl.BlockSpec((tm // t, A_WIDTH, t), lambda i, j: (i, 0, 0)),
            pl.BlockSpec((tm // t, B_LAT, t), lambda i, j: (i, 0, 0)),
            pl.BlockSpec((IDX_HEADS, tm), lambda i, j: (0, i)),
        ],
        out_shape=[
            jax.ShapeDtypeStruct((m, MAIN_W), BF16),
            jax.ShapeDtypeStruct((m, B_LAT), BF16),
            jax.ShapeDtypeStruct((m, 2 * IDX_DIM), BF16),
            jax.ShapeDtypeStruct((m // t, A_WIDTH, t), BF16),
            jax.ShapeDtypeStruct((m // t, B_LAT, t), BF16),
            jax.ShapeDtypeStruct((IDX_HEADS, m), F32),
        ],
        scratch_shapes=[pltpu.VMEM((tm, d), BF16)],
        compiler_params=pltpu.CompilerParams(
            dimension_semantics=("parallel", "arbitrary"), vmem_limit_bytes=56 * 2 ** 20),
        name="input_projection",
    )(x2, g, w_main, w_small, w_t, kv_g, kv_g.reshape(-1, 1), ik_g2)


def _attn_a_kernel(q_ref, k_ref, vt_ref, z_ref, bias_ref, lq1_ref, lk1_ref, lq2_ref, lk2_ref, g_ref,
                   o_ref, qs_ref, m_ref, acc_ref, *, lam_init, n_blocks):
    t = ATT_TILE
    lam = (jnp.exp(jnp.sum(lq1_ref[...] * lk1_ref[...], axis=-1, keepdims=True))
           - jnp.exp(jnp.sum(lq2_ref[...] * lk2_ref[...], axis=-1, keepdims=True)) + lam_init)
    lane = lax.broadcasted_iota(I32, (t, LANES), 1)
    zero = jnp.zeros((t, LANES), BF16)
    heads = range(A_GROUP)
    ones_rows = jnp.ones((PACK, t), BF16)

    def q_block(i, carry):
        rows = pl.ds(pl.multiple_of(i * t, t), t)
        for g in heads:
            q = q_ref[rows, g * LANES:(g + 1) * LANES]
            qs_ref[g, 0:t, :] = jnp.where(lane < A_QK_DIM, q, zero)
            qs_ref[g, t:2 * t, :] = jnp.where(lane >= A_QK_DIM, q, zero)
        m_ref[...] = jnp.full(m_ref.shape, M_INIT, F32)
        acc_ref[...] = jnp.zeros(acc_ref.shape, F32)

        def scores(g, j):
            keys = pl.ds(pl.multiple_of(j * t, t), t)
            return _dot_nt(k_ref[keys, g * LANES:(g + 1) * LANES], qs_ref[g])

        def softmax_step(g, j, s, near):
            vtj = jnp.concatenate([vt_ref[j, g * A_V_DIM:(g + 1) * A_V_DIM, :], ones_rows], axis=0)
            for c_ in range(2):
                cols = slice(c_ * t, (c_ + 1) * t)
                sc = s[:, cols]
                if near:
                    sc = sc + bias_ref[g, i - j]
                m_old = m_ref[g, :, cols]
                m_new = jnp.maximum(m_old, jnp.max(sc, axis=0, keepdims=True))
                alpha = jnp.exp(m_old - m_new)
                p = jnp.exp(sc - m_new)
                acc_ref[g, :, cols] = alpha * acc_ref[g, :, cols] + _dot(vtj, p.astype(BF16))
                m_ref[g, :, cols] = m_new

        def kv_tiles(j0, kinds):
            for c_, near in enumerate(kinds):
                s = [scores(g, j0 + c_) for g in heads]
                for g in heads:
                    softmax_step(g, j0 + c_, s[g], near)

        _causal_tile_runs(i, kv_tiles)

        for g in heads:
            inv_l = 1.0 / acc_ref[g, A_V_DIM:A_V_DIM + 1, :]
            a_t = (acc_ref[g, 0:A_V_DIM, 0:t] * inv_l[:, 0:t]
                   - lam * (acc_ref[g, 0:A_V_DIM, t:2 * t] * inv_l[:, t:2 * t]))
            y_t = a_t * lax.rsqrt(jnp.mean(a_t * a_t, axis=0, keepdims=True) + EPS)
            z = z_ref[rows, g * LANES:(g + 1) * LANES].astype(F32)
            o_ref[rows, g * LANES:(g + 1) * LANES] = (
                y_t.T * g_ref[...] * (1.0 - lam_init) * (z / (1.0 + jnp.exp(-z)))).astype(BF16)
        return carry

    lax.fori_loop(0, n_blocks, q_block, 0)


def _attention_a(main, v_t, bias_tiles_t, lq1, lk1, lq2, lk2, sub_g, batch, seq, lam_init):
    t = ATT_TILE
    nq = seq // t
    gw = A_GROUP * LANES
    vec = lambda n: pl.BlockSpec((1, n), lambda b, h: (0, 0))
    col = lambda off: pl.BlockSpec((seq, gw), lambda b, h: (b, off // A_GROUP + h))
    return pl.pallas_call(
        functools.partial(_attn_a_kernel, lam_init=lam_init, n_blocks=nq),
        grid=(batch, A_HEADS // A_GROUP),
        in_specs=[
            col(_QA), col(_KA),
            pl.BlockSpec((nq, A_GROUP * A_V_DIM, t), lambda b, h: (b, h, 0)),
            col(_ZA),
            pl.BlockSpec((A_GROUP, 2, t, t), lambda b, h: (h, 0, 0, 0)),
            vec(A_QK_DIM), vec(A_QK_DIM), vec(A_QK_DIM), vec(A_QK_DIM), vec(A_V_DIM),
        ],
        out_specs=pl.BlockSpec((seq, gw), lambda b, h: (b, h)),
        out_shape=jax.ShapeDtypeStruct((batch * seq, A_WIDTH), BF16),
        scratch_shapes=[
            pltpu.VMEM((A_GROUP, 2 * t, LANES), BF16),
            pltpu.VMEM((A_GROUP, 1, 2 * t), F32),
            pltpu.VMEM((A_GROUP, A_V_DIM + PACK, 2 * t), F32),
        ],
        compiler_params=pltpu.CompilerParams(
            dimension_semantics=("parallel", "arbitrary"), vmem_limit_bytes=56 * 2 ** 20),
        name="attention_a",
    )(main, main, v_t, main, bias_tiles_t, lq1, lk1, lq2, lk2, sub_g)


def _attn_b_kernel(qb_ref, iq_ref, iqn_ref, zb_ref, ckv_ref, ckvt_ref, ik_ref, iwt_ref, iwtn_ref, bias_ref,
                   wuvt_ref, o_ref, keys_ref, hi_ref, lo_ref, cand_ref, iqm_ref, m_ref, acc_ref,
                   j_ref, thr_ref, cnt_ref, *, topk, seq):
    t = ATT_TILE
    nq = seq // t
    i = pl.program_id(1)
    n_chunks = i + 1
    slot, nslot = i % 2, (i + 1) % 2
    cur = keys_ref.at[slot]
    cur_hi, cur_lo = hi_ref.at[slot], lo_ref.at[slot]

    lane = lax.broadcasted_iota(I32, (t, LANES), 1)
    zero = jnp.zeros((t, LANES), BF16)
    key_minus_query = (lax.broadcasted_iota(I32, (t, t), 0) - lax.broadcasted_iota(I32, (t, t), 1))
    key_row = lax.broadcasted_iota(I32, (t, t), 0)
    idx_scale = (IDX_HEADS ** -0.5) * (IDX_DIM ** -0.5)

    def stage_indexer_queries(src_ref):
        for hp in range(IDX_HEADS // 2):
            pair = src_ref[:, hp * LANES:(hp + 1) * LANES]
            iqm_ref[2 * hp] = jnp.where(lane < IDX_DIM, pair, zero)
            iqm_ref[2 * hp + 1] = jnp.where(lane >= IDX_DIM, pair, zero)

    def index_keys(j, wt, dst_slot, diagonal):
        rows = pl.ds(pl.multiple_of(j * t, t), t)
        ikj = ik_ref[rows, :]
        score = jnp.zeros((t, t), F32)
        for h in range(IDX_HEADS):
            d = _dot_nt(ikj, iqm_ref[h])
            score = score + wt[h:h + 1, :] * jnp.maximum(d, 0.0)
        bits = lax.bitcast_convert_type(score, I32)
        key = bits ^ ((bits >> 31) & np.int32(0x7FFFFFFF))
        if diagonal:
            key = jnp.where(key_minus_query <= 0, key, INT_MIN)
        keys_ref[dst_slot, rows, :] = key
        hi_ref[dst_slot, rows, :] = (key >> 16).astype(I16)
        lo_ref[dst_slot, rows, :] = ((key & np.int32(0xFFFF)) - np.int32(HALF)).astype(I16)

    @pl.when(i == 0)
    def _():
        stage_indexer_queries(iq_ref)
        index_keys(0, iwt_ref[...] * idx_scale, slot, True)

    has_next = i < nq - 1

    @pl.when(has_next)
    def _():
        stage_indexer_queries(iqn_ref)

    wt_next = iwtn_ref[...] * idx_scale

    def count(pred_fn):
        def body(j, c8):
            kj = cur[pl.ds(pl.multiple_of(j * t, t), t), :]
            return c8 + _rows8_sum(pred_fn(kj, j).astype(I32))
        c8 = lax.fori_loop(0, n_chunks, body, jnp.zeros((8, t), I32))
        return jnp.sum(c8, axis=0, keepdims=True)

    def count16(ref, n_static, trial):
        one, nil = jnp.ones((), BF16), jnp.zeros((), BF16)
        trial16 = trial.astype(I16)
        acc = None
        for r in range(n_static * t // PACK):
            hit = jnp.where(ref[r * PACK:(r + 1) * PACK, :] >= trial16, one, nil)
            acc = hit if acc is None else acc + hit
        return jnp.sum(acc.astype(F32), axis=0, keepdims=True).astype(I32)

    def bisect16(ref, n_static, need, cnt_init):
        def bit_step(b, carry):
            cu, cnt_at = carry
            trial_u = cu | lax.shift_left(np.int32(1), np.int32(15) - b)
            cnt = count16(ref, n_static, trial_u - HALF)
            ok = cnt >= need
            return jnp.where(ok, trial_u, cu), jnp.where(ok, cnt, cnt_at)

        cu, cnt = lax.fori_loop(0, 16, bit_step, (jnp.zeros((1, t), I32), cnt_init))
        return cu - HALF, cnt

    def bisect(n_static):
        assert n_static * t // PACK <= 256
        thr_hi, cnt_ge_hi = bisect16(cur_hi, n_static, topk, jnp.zeros((1, t), I32))
        cnt_gt_hi = count16(cur_hi, n_static, jnp.minimum(thr_hi + 1, HALF - 1))
        cnt_gt_hi = jnp.where(thr_hi == HALF - 1, 0, cnt_gt_hi)
        thr_hi16 = thr_hi.astype(I16)
        for j in range(n_static):
            rows = slice(j * t, (j + 1) * t)
            cand_ref[rows, :] = jnp.where(cur_hi[rows, :] == thr_hi16, cur_lo[rows, :], np.int16(-HALF))
        thr_lo, cnt_lo = bisect16(cand_ref, n_static, topk - cnt_gt_hi, cnt_ge_hi - cnt_gt_hi)
        thr_ref[...] = thr_hi * (2 * HALF) + (thr_lo + HALF)
        cnt_ref[...] = cnt_gt_hi + cnt_lo

    for n_static in range(1, nq + 1):
        pl.when(i == n_static - 1)(functools.partial(bisect, n_static))

    thr = thr_ref[...]
    cnt_ge = cnt_ref[...]
    thr_eq = jnp.maximum(thr, INT_MIN + np.int32(1))

    idx_bits = max(1, (seq - 1).bit_length())
    j_ref[...] = jnp.full((1, t), 2 ** idx_bits - 1, I32)

    @pl.when(jnp.max((cnt_ge - topk).astype(F32)) > 0.0)
    def _():
        need = topk - count(lambda kj, j: kj > thr)

        def pos_step(b, jlim):
            trial = jlim + lax.shift_left(np.int32(1), np.int32(idx_bits - 1) - b)
            cnt = count(lambda kj, j: (kj == thr_eq) & (key_row + j * t < trial))
            return jnp.where(cnt < need, trial, jlim)

        j_ref[...] = lax.fori_loop(0, idx_bits, pos_step, jnp.zeros((1, t), I32))

    j_lim = j_ref[...]

    m_ref[...] = jnp.full(m_ref.shape, M_INIT, F32)
    acc_ref[...] = jnp.zeros(acc_ref.shape, F32)
    ones_rows = jnp.ones((PACK, t), BF16)

    def att_tiles(with_indexer):
        def run(j0, kinds):
            for c_, near in enumerate(kinds):
                j = j0 + c_
                rows = pl.ds(pl.multiple_of(j * t, t), t)
                kj = cur[rows, :]
                sel = (kj > thr) | ((kj == thr_eq) & (key_row + j * t <= j_lim))
                sel_bias = jnp.where(sel, 0.0, -jnp.inf)
                ckvj = ckv_ref[rows, :]
                ckvtj = jnp.concatenate([ckvt_ref[j], ones_rows], axis=0)
                raw = [_dot_nt(ckvj, qb_ref[:, h * B_LAT:(h + 1) * B_LAT]) for h in range(B_HEADS)]
                if with_indexer:
                    index_keys(j, wt_next, nslot, False)
                    if c_ == 0 and kinds[-1]:
                        index_keys(j0 + len(kinds), wt_next, nslot, True)
                ps, alphas = [], []
                for h in range(B_HEADS):
                    lg = raw[h] + sel_bias
                    if near:
                        lg = lg + bias_ref[h, i - j]
                    m_old = m_ref[h]
                    m_new = jnp.maximum(m_old, jnp.max(lg, axis=0, keepdims=True))
                    alpha = jnp.exp(m_old - m_new)
                    p = jnp.exp(lg - m_new)
                    m_ref[h] = m_new
                    ps.append(p.astype(BF16))
                    alphas.append(alpha)
                pvs = [_dot(ckvtj, ps[h]) for h in range(B_HEADS)]
                for h in range(B_HEADS):
                    acc_ref[h] = alphas[h] * acc_ref[h] + pvs[h]
        return run

    @pl.when(has_next)
    def _():
        _causal_tile_runs(i, att_tiles(True))

    @pl.when(jnp.logical_not(has_next))
    def _():
        _causal_tile_runs(i, att_tiles(False))

    for h in range(B_HEADS):
        inv_l = 1.0 / acc_ref[h, B_LAT:B_LAT + 1, :]
        o_t = _dot(wuvt_ref[h], acc_ref[h, 0:B_LAT, :].astype(BF16)) * inv_l
        z = zb_ref[:, h * B_V_DIM:(h + 1) * B_V_DIM].astype(F32)
        o_ref[:, h * B_V_DIM:(h + 1) * B_V_DIM] = (o_t.T * (z / (1.0 + jnp.exp(-z)))).astype(BF16)


def _attention_b(main, ckv, ckv_t, ik2, iw_t, bias_tiles_t, w_uv_t, batch, seq, topk):
    t = ATT_TILE
    nq = seq // t
    tile = lambda b, i: b * nq + i
    next_tile = lambda b, i: b * nq + jnp.minimum(i + 1, nq - 1)
    iq_col = _IQ * LANES // (IDX_HEADS * IDX_DIM)
    return pl.pallas_call(
        functools.partial(_attn_b_kernel, topk=topk, seq=seq),
        grid=(batch, nq),
        in_specs=[
            pl.BlockSpec((t, B_HEADS * B_LAT), lambda b, i: (tile(b, i), _QB * LANES // (B_HEADS * B_LAT))),
            pl.BlockSpec((t, IDX_HEADS * IDX_DIM), lambda b, i: (tile(b, i), iq_col)),
            pl.BlockSpec((t, IDX_HEADS * IDX_DIM), lambda b, i: (next_tile(b, i), iq_col)),
            pl.BlockSpec((t, B_WIDTH), lambda b, i: (tile(b, i), _ZB * LANES // B_WIDTH)),
            pl.BlockSpec((seq, B_LAT), lambda b, i: (b, 0)),
            pl.BlockSpec((nq, B_LAT, t), lambda b, i: (b, 0, 0)),
            pl.BlockSpec((seq, 2 * IDX_DIM), lambda b, i: (b, 0)),
            pl.BlockSpec((IDX_HEADS, t), lambda b, i: (0, tile(b, i))),
            pl.BlockSpec((IDX_HEADS, t), lambda b, i: (0, next_tile(b, i))),
            pl.BlockSpec((B_HEADS, 2, t, t), lambda b, i: (0, 0, 0, 0)),
            pl.BlockSpec((B_HEADS, B_V_DIM, B_LAT), lambda b, i: (0, 0, 0)),
        ],
        out_specs=pl.BlockSpec((t, B_WIDTH), lambda b, i: (tile(b, i), 0)),
        out_shape=jax.ShapeDtypeStruct((batch * seq, B_WIDTH), BF16),
        scratch_shapes=[
            pltpu.VMEM((2, seq, t), I32),
            pltpu.VMEM((2, seq, t), I16),
            pltpu.VMEM((2, seq, t), I16),
            pltpu.VMEM((seq, t), I16),
            pltpu.VMEM((IDX_HEADS, t, LANES), BF16),
            pltpu.VMEM((B_HEADS, 1, t), F32),
            pltpu.VMEM((B_HEADS, B_LAT + PACK, t), F32),
            pltpu.VMEM((1, t), I32),
            pltpu.VMEM((1, t), I32),
            pltpu.VMEM((1, t), I32),
        ],
        compiler_params=pltpu.CompilerParams(
            dimension_semantics=("parallel", "arbitrary"), vmem_limit_bytes=52 * 2 ** 20),
        name="attention_b",
    )(main, main, main, main, ckv, ckv_t, ik2, iw_t, iw_t, bias_tiles_t, w_uv_t)


def _out_kernel(oa_ref, ob_ref, x_ref, w_ref, g_ref, o_ref):
    y = _dot(oa_ref[...], w_ref[0:A_WIDTH, :]) + _dot(ob_ref[...], w_ref[A_WIDTH:A_WIDTH + B_WIDTH, :])
    ms = jnp.mean(y * y, axis=-1, keepdims=True)
    o_ref[...] = x_ref[...] + y * lax.rsqrt(ms + EPS) * g_ref[...]


def _output_projection(oa, ob, x2, w_out, g):
    m, d = x2.shape
    tm = min(OUT_TM, m)
    return pl.pallas_call(
        _out_kernel,
        grid=(m // tm,),
        in_specs=[
            pl.BlockSpec((tm, A_WIDTH), lambda i: (i, 0)),
            pl.BlockSpec((tm, B_WIDTH), lambda i: (i, 0)),
            pl.BlockSpec((tm, d), lambda i: (i, 0)),
            pl.BlockSpec((A_WIDTH + B_WIDTH, d), lambda i: (0, 0)),
            pl.BlockSpec((1, d), lambda i: (0, 0)),
        ],
        out_specs=pl.BlockSpec((tm, d), lambda i: (i, 0)),
        out_shape=jax.ShapeDtypeStruct((m, d), F32),
        compiler_params=pltpu.CompilerParams(
            dimension_semantics=("parallel",), vmem_limit_bytes=52 * 2 ** 20),
        name="output_projection",
    )(oa, ob, x2, w_out, g)


def _t5_bucket(dist):
    n = jnp.maximum(dist, 0)
    max_exact = N_BUCKETS // 2
    nf = jnp.maximum(n, 1).astype(F32)
    large = max_exact + (jnp.log(nf / max_exact) / math.log(MAX_DISTANCE / max_exact)
                         * (N_BUCKETS - max_exact)).astype(I32)
    large = jnp.minimum(large, N_BUCKETS - 1)
    return jnp.where(n < max_exact, n, large)


def _near_bias_tiles_t(bias_tab):
    t = ATT_TILE
    assert t > MAX_DISTANCE
    n_heads = bias_tab.shape[1]
    by_dist = bias_tab[_t5_bucket(jnp.arange(-t, 2 * t + 1))].T
    by_dist = by_dist - by_dist[:, 3 * t:]
    tiles = []
    for d in range(2):
        u = by_dist[:, d * t:d * t + 2 * t]
        flat = jnp.tile(u, (1, t + 1))[:, t:t + t * (2 * t - 1)]
        tiles.append(flat.reshape(n_heads, t, 2 * t - 1)[:, :, :t])
    return jnp.stack(tiles, axis=1)


def _causal_masked(tiles):
    t = ATT_TILE
    k = jnp.arange(t)[:, None]
    q = jnp.arange(t)[None, :]
    dist = jnp.stack([d * t + q - k for d in range(2)])
    return jnp.where(dist[None] >= 0, tiles, -jnp.inf)


def _split_w_in(w):
    d = w.shape[0]
    sizes = (2 * A_HEADS * A_QK_DIM, 2 * A_HEADS * A_QK_DIM, A_WIDTH, A_WIDTH, B_HEADS * B_LAT, B_LAT,
             B_WIDTH, IDX_HEADS * IDX_DIM, IDX_DIM, IDX_HEADS)
    assert sum(sizes) == w.shape[1]
    offs = np.cumsum((0,) + sizes)
    qa, ka, va, za, qb, ckv, zb, iq, ik, iw = (w[:, offs[n]:offs[n + 1]] for n in range(len(sizes)))

    def by_head(m):
        return m.reshape(d, 2, A_HEADS, A_QK_DIM).transpose(0, 2, 1, 3).reshape(d, 2 * A_HEADS * A_QK_DIM)

    main = jnp.concatenate([qb * (B_LAT ** -0.5), by_head(qa) * (A_QK_DIM ** -0.5), by_head(ka), za, zb, iq],
                           axis=1)
    small = jnp.concatenate([ckv, ik, ik], axis=1)
    transposed = jnp.concatenate([va, ckv, iw], axis=1).astype(BF16).T
    assert main.shape[1] == MAIN_W and small.shape[1] == SMALL_W
    return main.astype(BF16), small.astype(BF16), transposed


def kernel(x, norm_pre_g, w_in, lambda_q1, lambda_k1, lambda_q2, lambda_k2, subln_g, kv_norm_g, idx_k_norm_g,
           w_uv, rel_bias, w_out, norm_post_g):
    batch, seq, d_model = x.shape
    t = ATT_TILE
    assert seq % t == 0 and d_model % LANES == 0
    topk = min(TOPK_MAX, seq // 4)
    row = lambda v: v.reshape(1, -1).astype(F32)

    bias_t = _near_bias_tiles_t(rel_bias)
    bias_a_t = _causal_masked(bias_t[:A_HEADS])
    bias_b_t = bias_t[A_HEADS:]

    x2 = x.reshape(batch * seq, d_model)
    for l in range(w_in.shape[0]):
        lam_init = 0.8 - 0.6 * math.exp(-0.3 * l)
        w_main, w_small, w_t = _split_w_in(w_in[l])
        ik_g2 = jnp.concatenate([idx_k_norm_g[l], idx_k_norm_g[l]])
        main, ckv, ik2, v_t, ckv_t, iw_t = _input_projection(
            x2, row(norm_pre_g[l]), w_main, w_small, w_t, row(kv_norm_g[l]), row(ik_g2))
        o_a = _attention_a(main, v_t, bias_a_t, row(lambda_q1[l]), row(lambda_k1[l]), row(lambda_q2[l]),
                           row(lambda_k2[l]), row(subln_g[l]), batch, seq, lam_init)
        w_uv_t = jnp.swapaxes(w_uv[l], 1, 2).astype(BF16)
        o_b = _attention_b(main, ckv, ckv_t, ik2, iw_t, bias_b_t, w_uv_t, batch, seq, topk)
        x2 = _output_projection(o_a, o_b, x2, w_out[l].astype(BF16), row(norm_post_g[l]))
    return x2.reshape(batch, seq, d_model)
```
